```python
import math
import jax
import jax.numpy as jnp
from jax import lax
import numpy as np

D_MODEL = 2048
BATCH = 4
SEQ = 4096
DEPTH = 2

CTX_LEN = 256
GRID_W = 64
EPS = 1e-6
N_MOD = 6

NA_HEADS = 8
NA_HEAD_DIM = 128
NA_WIN_H = 8
NA_WIN_W = 16
NA_QBLK = NA_WIN_W
NA_KBLK = 2 * NA_WIN_W

ML_HEADS = 4
ML_QK_DIM = 128
ML_V_DIM = 256
ML_CHUNK = 64

HG_HEADS = 8
HG_K_DIM = 128
HG_V_DIM = 128
HG_CHUNK = 32

SSD_HEADS = 16
SSD_HEAD_DIM = 64
SSD_GROUPS = 2
SSD_STATE = 128
SSD_CONV = 5
SSD_CHUNK = 64

FFN_MULT = 256
FFN_HIDDEN = ((8 * D_MODEL + 3 * FFN_MULT - 1) // (3 * FFN_MULT)) * FFN_MULT

NA_W = NA_HEADS * NA_HEAD_DIM
ML_QK_W = ML_HEADS * ML_QK_DIM
ML_W = ML_HEADS * ML_V_DIM
HG_K_W = HG_HEADS * HG_K_DIM
HG_W = HG_HEADS * HG_V_DIM
SSD_W = SSD_HEADS * SSD_HEAD_DIM
SSD_BC_W = SSD_GROUPS * SSD_STATE
SSD_XBC = SSD_W + 2 * SSD_BC_W

EVEN_SPLIT = (NA_W, NA_W, NA_W, ML_QK_W, ML_QK_W, ML_W, ML_W, 4 * ML_HEADS)
ODD_SPLIT = (HG_K_W, HG_K_W, HG_K_W, HG_W, HG_W, SSD_W, SSD_XBC, 2 * SSD_HEADS)
EVEN_IN = sum(EVEN_SPLIT)
ODD_IN = sum(ODD_SPLIT)
EVEN_MIX = NA_W + ML_W
ODD_MIX = HG_W + SSD_W

kernel_name = 'hybrid_natten_mlstm_hgrn2_ssd_flow_block'


def rms_norm(x):
    xf = x.astype(jnp.float32)
    y = xf * lax.rsqrt(jnp.mean(xf * xf, axis=-1, keepdims=True) + EPS)
    return y.astype(x.dtype)


def modulate(x, shift, scale):
    return rms_norm(x) * (1 + scale) + shift


def modulation(cvec, mod_w, mod_b):
    return jnp.split(jax.nn.silu(cvec) @ mod_w + mod_b, N_MOD, axis=-1)


def split_cols(p, sizes):
    return jnp.split(p, [int(v) for v in np.cumsum(sizes)[:-1]], axis=-1)


def heads(t, n):
    b, T, _ = t.shape
    return t.reshape(b, T, n, -1).transpose(0, 2, 1, 3)


def merge_heads(t):
    b, n, T, d = t.shape
    return t.transpose(0, 2, 1, 3).reshape(b, T, n * d)


def swiglu(h, w1, w3, w2):
    return (jax.nn.silu(h @ w1) * (h @ w3)) @ w2


def centred_conv(x, w, b):
    k, ch = w.shape
    y = lax.conv_general_dilated(x, w[:, None, :], window_strides=(1,), padding=[(k // 2, k // 2)],
                                 dimension_numbers=('NWC', 'WIO', 'NWC'), feature_group_count=ch)
    return y + b


def chunked_scan(step, xs, state0, chunk):
    T = xs[0].shape[2]
    nc = T // chunk

    def to_chunks(a):
        a = a.reshape(a.shape[:2] + (nc, chunk) + a.shape[3:])
        return jnp.moveaxis(a, 2, 0)

    state, ys = lax.scan(step, state0, tuple(to_chunks(a) for a in xs))
    ys = jnp.moveaxis(ys, 0, 2)
    return ys.reshape(ys.shape[:2] + (T,) + ys.shape[4:]), state


def bidirectional(step, chunk, state0, ctx_fwd, lat_fwd, ctx_bwd, lat_bwd):
    yc_f, st_f = chunked_scan(step, ctx_fwd, state0, chunk)
    yl_f, _ = chunked_scan(step, lat_fwd, st_f, chunk)
    flip = lambda xs: tuple(jnp.flip(a, 2) for a in xs)
    yc_b, st_b = chunked_scan(step, flip(ctx_bwd), state0, chunk)
    yl_b, _ = chunked_scan(step, flip(lat_bwd), st_b, chunk)
    return yc_f + jnp.flip(yc_b, 2), yl_f + jnp.flip(yl_b, 2)


def mlstm_chunk(carry, xs):
    C, nrm, m = carry
    q, k, v, li, lf = xs
    L = q.shape[2]
    causal = jnp.tril(jnp.ones((L, L), bool))
    bcum = jnp.cumsum(lf, axis=-1)
    dlog = jnp.where(causal, bcum[..., :, None] - bcum[..., None, :] + li[..., None, :], -jnp.inf)
    inter = bcum + m[..., None]
    m_t = jnp.maximum(inter, jnp.max(dlog, axis=-1))
    w_inter = jnp.exp(inter - m_t)
    s = jnp.einsum('bhtk,bhsk->bhts', q, k) * jnp.exp(dlog - m_t[..., None])
    num = w_inter[..., None] * jnp.einsum('bhvk,bhtk->bhtv', C, q) + jnp.einsum('bhts,bhsv->bhtv', s, v)
    den = w_inter * jnp.einsum('bhk,bhtk->bht', nrm, q) + jnp.sum(s, axis=-1)
    h = num / jnp.maximum(jnp.abs(den), jnp.exp(-m_t))[..., None]
    b_last = bcum[..., -1]
    g = b_last[..., None] - bcum + li
    m_new = jnp.maximum(b_last + m, jnp.max(g, axis=-1))
    a = jnp.exp(b_last + m - m_new)
    w = jnp.exp(g - m_new[..., None])
    C_new = a[..., None, None] * C + jnp.einsum('bhs,bhsv,bhsk->bhvk', w, v, k)
    n_new = a[..., None] * nrm + jnp.einsum('bhs,bhsk->bhk', w, k)
    return (C_new, n_new, m_new), h


def hgrn_chunk(S, xs):
    q, k, lg, v = xs
    L = q.shape[2]
    causal = jnp.tril(jnp.ones((L, L), bool))
    G = jnp.cumsum(lg, axis=2)
    diff = G[:, :, :, None, :] - G[:, :, None, :, :]
    dec = jnp.exp(jnp.where(causal[:, :, None], diff, -jnp.inf))
    A = jnp.einsum('bhtk,bhsk,bhtsk->bhts', q, k, dec)
    o = jnp.einsum('bhts,bhsv->bhtv', A, v) + jnp.einsum('bhtk,bhkv->bhtv', q * jnp.exp(G), S)
    GL = G[:, :, -1:, :]
    S_new = jnp.exp(GL[:, :, 0])[..., None] * S + jnp.einsum('bhsk,bhsv->bhkv', k * jnp.exp(GL - G), v)
    return S_new, o


def ssd_chunk(S, xs):
    x, dt, a, Bm, Cm = xs
    nh = x.shape[1]
    L = x.shape[2]
    hpg = nh // Bm.shape[1]
    causal = jnp.tril(jnp.ones((L, L), bool))
    Bh = jnp.repeat(Bm, hpg, axis=1)
    Ch = jnp.repeat(Cm, hpg, axis=1)
    acum = jnp.cumsum(a, axis=-1)
    decay = jnp.exp(jnp.where(causal, acum[..., :, None] - acum[..., None, :], -jnp.inf))
    cb = jnp.einsum('bhtn,bhsn->bhts', Ch, Bh) * decay * dt[..., None, :]
    y = jnp.einsum('bhts,bhsp->bhtp', cb, x) + jnp.einsum('bhtn,bhpn->bhtp', Ch * jnp.exp(acum)[..., None], S)
    w = jnp.exp(acum[..., -1:] - acum) * dt
    S_new = jnp.exp(acum[..., -1])[..., None, None] * S + jnp.einsum('bhsn,bhs,bhsp->bhpn', Bh, w, x)
    return S_new, y


def neighbourhood_attention(q, k, v, qc, kc, vc, rpb):
    b, h, n, dh = q.shape
    rows = n // GRID_W
    win_h = min(NA_WIN_H, rows)
    ncb = GRID_W // NA_QBLK
    scale = dh ** -0.5
    f32 = jnp.float32
    qcol = np.arange(GRID_W).reshape(ncb, NA_QBLK)
    kstart = np.clip(np.arange(ncb) * NA_QBLK - NA_WIN_W // 2, 0, GRID_W - NA_KBLK)
    kcol = kstart[:, None] + np.arange(NA_KBLK)
    wstart = np.clip(qcol - NA_WIN_W // 2, 0, GRID_W - NA_WIN_W)
    col_ok = (kcol[:, None, :] >= wstart[..., None]) & (kcol[:, None, :] < wstart[..., None] + NA_WIN_W)
    mask = jnp.asarray(np.broadcast_to(col_ok[:, :, None, :], (ncb, NA_QBLK, win_h, NA_KBLK))
                       .reshape(ncb, NA_QBLK, win_h * NA_KBLK))
    dcol = np.clip(kcol[:, None, :] - qcol[..., None], -(NA_WIN_W - 1), NA_WIN_W - 1) + NA_WIN_W - 1
    qg = q.reshape(b, h, rows, ncb, NA_QBLK, dh)
    kg = k.reshape(b, h, rows, GRID_W, dh)
    vg = v.reshape(b, h, rows, GRID_W, dh)
    rpb32 = rpb.astype(f32)

    def row_block(r):
        r0 = jnp.clip(r - win_h // 2, 0, rows - win_h)
        qr = lax.dynamic_index_in_dim(qg, r, axis=2, keepdims=False)

        def gather_band(t):
            band = lax.dynamic_slice_in_dim(t, r0, win_h, axis=2)[:, :, :, kcol]
            return band.transpose(0, 1, 3, 2, 4, 5).reshape(b, h, ncb, win_h * NA_KBLK, dh)

        kb = gather_band(kg)
        vb = gather_band(vg)
        drow = r0 + jnp.arange(win_h) - r + NA_WIN_H - 1
        bias = rpb32[:, drow][:, :, dcol]
        bias = bias.transpose(0, 2, 3, 1, 4).reshape(h, ncb, NA_QBLK, win_h * NA_KBLK)
        s_loc = jnp.einsum('bhjqd,bhjkd->bhjqk', qr, kb).astype(f32) * scale + bias[None]
        s_loc = jnp.where(mask, s_loc, -jnp.inf)
        s_ctx = jnp.einsum('bhjqd,bhcd->bhjqc', qr, kc).astype(f32) * scale
        p = jax.nn.softmax(jnp.concatenate([s_loc, s_ctx], axis=-1), axis=-1).astype(v.dtype)
        nk = s_loc.shape[-1]
        return (jnp.einsum('bhjqk,bhjkd->bhjqd', p[..., :nk], vb)
                + jnp.einsum('bhjqc,bhcd->bhjqd', p[..., nk:], vc))

    out = lax.map(row_block, jnp.arange(rows))
    out_lat = out.transpose(1, 2, 0, 3, 4, 5).reshape(b, h, n, dh)
    s_c = jnp.einsum('bhqd,bhkd->bhqk', qc, kc).astype(f32) * scale
    out_ctx = jnp.einsum('bhqk,bhkd->bhqd', jax.nn.softmax(s_c, axis=-1).astype(vc.dtype), vc)
    return out_lat, out_ctx


def mixer_na_mlstm(h_lat, h_ctx, w_in, q_gain, k_gain, rpb, gate_bias, ml_gain):
    dtype = h_lat.dtype
    f32 = jnp.float32
    lat = split_cols(h_lat @ w_in, EVEN_SPLIT)
    ctx = split_cols(h_ctx @ w_in, EVEN_SPLIT)
    qkn = lambda t, g: rms_norm(heads(t, NA_HEADS)) * g
    na_lat, na_ctx = neighbourhood_attention(
        qkn(lat[0], q_gain), qkn(lat[1], k_gain), heads(lat[2], NA_HEADS),
        qkn(ctx[0], q_gain), qkn(ctx[1], k_gain), heads(ctx[2], NA_HEADS), rpb)

    def ml_inputs(p):
        q = heads(p[3], ML_HEADS).astype(f32) * ML_QK_DIM ** -0.5
        k = heads(p[4], ML_HEADS).astype(f32)
        v = heads(p[5], ML_HEADS).astype(f32)
        g = (p[7] + gate_bias.reshape(-1)).astype(f32)
        bb, T, _ = g.shape
        g = g.reshape(bb, T, 4, ML_HEADS).transpose(2, 0, 3, 1)
        fwd = (q, k, v, g[0], jax.nn.log_sigmoid(g[1]))
        bwd = (q, k, v, g[2], jax.nn.log_sigmoid(g[3]))
        return fwd, bwd

    c_fwd, c_bwd = ml_inputs(ctx)
    l_fwd, l_bwd = ml_inputs(lat)
    b = h_lat.shape[0]
    state0 = (jnp.zeros((b, ML_HEADS, ML_V_DIM, ML_QK_DIM), f32),
              jnp.zeros((b, ML_HEADS, ML_QK_DIM), f32),
              jnp.zeros((b, ML_HEADS), f32))
    ml_ctx, ml_lat = bidirectional(mlstm_chunk, ML_CHUNK, state0, c_fwd, l_fwd, c_bwd, l_bwd)
    ml_out = lambda hs, p: merge_heads(rms_norm(hs)).astype(dtype) * ml_gain * jax.nn.sigmoid(p[6])
    y_lat = jnp.concatenate([merge_heads(na_lat), ml_out(ml_lat, lat)], axis=-1)
    y_ctx = jnp.concatenate([merge_heads(na_ctx), ml_out(ml_ctx, ctx)], axis=-1)
    return y_lat, y_ctx


def mixer_hgrn_ssd(h_lat, h_ctx, lower_bound, w_in, hg_gain, conv_w, conv_b, dt_bias, a_log, d_skip, ssd_gain):
    dtype = h_lat.dtype
    f32 = jnp.float32
    lat = split_cols(h_lat @ w_in, ODD_SPLIT)
    ctx = split_cols(h_ctx @ w_in, ODD_SPLIT)
    b = h_lat.shape[0]
    lbound = lower_bound.reshape(HG_HEADS, 1, HG_K_DIM).astype(f32)

    def hg_inputs(p):
        q = jax.nn.silu(heads(p[0], HG_HEADS).astype(f32))
        v = jax.nn.silu(heads(p[3], HG_HEADS).astype(f32))

        def one_dir(fp):
            f = lbound + (1 - lbound) * jax.nn.sigmoid(heads(fp, HG_HEADS).astype(f32))
            return (q, 1 - f, jnp.log(f), v)

        return one_dir(p[1]), one_dir(p[2])

    c_fwd, c_bwd = hg_inputs(ctx)
    l_fwd, l_bwd = hg_inputs(lat)
    hg0 = jnp.zeros((b, HG_HEADS, HG_K_DIM, HG_V_DIM), f32)
    hg_ctx, hg_lat = bidirectional(hgrn_chunk, HG_CHUNK, hg0, c_fwd, l_fwd, c_bwd, l_bwd)
    hg_out = lambda o, p: merge_heads(rms_norm(o)).astype(dtype) * hg_gain * jax.nn.silu(p[4])
    A = -jnp.exp(a_log.astype(f32))

    def ssd_inputs(p):
        xbc = jax.nn.silu(centred_conv(p[6], conv_w, conv_b))
        xs, bm, cm = split_cols(xbc, (SSD_W, SSD_BC_W, SSD_BC_W))
        xh = heads(xs, SSD_HEADS).astype(f32)
        bm = heads(bm, SSD_GROUPS).astype(f32)
        cm = heads(cm, SSD_GROUPS).astype(f32)
        bb, T, _ = p[7].shape
        dts = jax.nn.softplus(p[7].astype(f32).reshape(bb, T, 2, SSD_HEADS) + dt_bias.astype(f32))
        dts = dts.transpose(2, 0, 3, 1)
        fwd = (xh, dts[0], dts[0] * A[0][:, None], bm, cm)
        bwd = (xh, dts[1], dts[1] * A[1][:, None], bm, cm)
        return fwd, bwd, xh

    sc_fwd, sc_bwd, xh_ctx = ssd_inputs(ctx)
    sl_fwd, sl_bwd, xh_lat = ssd_inputs(lat)
    ssd0 = jnp.zeros((b, SSD_HEADS, SSD_HEAD_DIM, SSD_STATE), f32)
    ys_ctx, ys_lat = bidirectional(ssd_chunk, SSD_CHUNK, ssd0, sc_fwd, sl_fwd, sc_bwd, sl_bwd)

    def ssd_out(ysum, xh, z):
        y = merge_heads(ysum + d_skip.astype(f32)[:, None, None] * xh) * jax.nn.silu(z.astype(f32))
        bb, T, _ = y.shape
        y = rms_norm(y.reshape(bb, T, SSD_GROUPS, -1)).reshape(bb, T, SSD_W)
        return y.astype(dtype) * ssd_gain

    y_lat = jnp.concatenate([hg_out(hg_lat, lat), ssd_out(ys_lat, xh_lat, lat[5])], axis=-1)
    y_ctx = jnp.concatenate([hg_out(hg_ctx, ctx), ssd_out(ys_ctx, xh_ctx, ctx[5])], axis=-1)
    return y_lat, y_ctx


def setup_inputs(seed: int = 0) -> dict:
    key = jax.random.key(seed)
    ks = iter(jax.random.split(key, 48))
    f32 = jnp.float32
    nrm = lambda shape, scale: jax.random.normal(next(ks), shape, f32) * scale
    gain = lambda n: 1.0 + 0.02 * jax.random.normal(next(ks), (n,), f32)
    d = D_MODEL
    inp = {}
    inp['x'] = nrm((BATCH, SEQ, d), 1.0)
    inp['c'] = nrm((BATCH, d), 1.0)
    inp['ctx'] = nrm((BATCH, CTX_LEN, d), 1.0)
    inp['c_ctx'] = nrm((d,), 1.0)
    inp['hgrn_lb_logits'] = nrm((DEPTH, HG_K_W), 0.1)
    inp['l0_mod_w'] = nrm((d, N_MOD * d), d ** -0.5)
    inp['l0_mod_b'] = nrm((N_MOD * d,), 0.02)
    inp['l0_w_in'] = nrm((d, EVEN_IN), d ** -0.5)
    inp['l0_q_gain'] = gain(NA_HEAD_DIM)
    inp['l0_k_gain'] = gain(NA_HEAD_DIM)
    inp['l0_rpb'] = nrm((NA_HEADS, 2 * NA_WIN_H - 1, 2 * NA_WIN_W - 1), 0.1)
    inp['l0_gate_bias'] = jnp.array([[0.0], [3.0], [0.0], [3.0]], f32) + nrm((4, ML_HEADS), 0.1)
    inp['l0_ml_gain'] = gain(ML_W)
    inp['l0_w_out'] = nrm((EVEN_MIX, d), EVEN_MIX ** -0.5)
    inp['l0_ffn_w1'] = nrm((d, FFN_HIDDEN), d ** -0.5)
    inp['l0_ffn_w3'] = nrm((d, FFN_HIDDEN), d ** -0.5)
    inp['l0_ffn_w2'] = nrm((FFN_HIDDEN, d), FFN_HIDDEN ** -0.5)
    inp['l1_mod_w'] = nrm((d, N_MOD * d), d ** -0.5)
    inp['l1_mod_b'] = nrm((N_MOD * d,), 0.02)
    inp['l1_w_in'] = nrm((d, ODD_IN), d ** -0.5)
    inp['l1_hg_gain'] = gain(HG_W)
    inp['l1_conv_w'] = nrm((SSD_CONV, SSD_XBC), SSD_CONV ** -0.5)
    inp['l1_conv_b'] = nrm((SSD_XBC,), 0.02)
    u = jax.random.uniform(next(ks), (2, SSD_HEADS), f32)
    dt0 = jnp.exp(u * (math.log(0.1) - math.log(1e-3)) + math.log(1e-3))
    inp['l1_dt_bias'] = dt0 + jnp.log(-jnp.expm1(-dt0))
    inp['l1_a_log'] = jnp.log(jax.random.uniform(next(ks), (2, SSD_HEADS), f32, minval=1.0, maxval=16.0))
    inp['l1_d_skip'] = 1.0 + nrm((SSD_HEADS,), 0.1)
    inp['l1_ssd_gain'] = gain(SSD_W)
    inp['l1_w_out'] = nrm((ODD_MIX, d), ODD_MIX ** -0.5)
    inp['l1_ffn_w1'] = nrm((d, FFN_HIDDEN), d ** -0.5)
    inp['l1_ffn_w3'] = nrm((d, FFN_HIDDEN), d ** -0.5)
    inp['l1_ffn_w2'] = nrm((FFN_HIDDEN, d), FFN_HIDDEN ** -0.5)
    return inp


def reference(x, c, ctx, c_ctx, hgrn_lb_logits,
              l0_mod_w, l0_mod_b, l0_w_in, l0_q_gain, l0_k_gain, l0_rpb, l0_gate_bias, l0_ml_gain,
              l0_w_out, l0_ffn_w1, l0_ffn_w3, l0_ffn_w2,
              l1_mod_w, l1_mod_b, l1_w_in, l1_hg_gain, l1_conv_w, l1_conv_b, l1_dt_bias, l1_a_log,
              l1_d_skip, l1_ssd_gain, l1_w_out, l1_ffn_w1, l1_ffn_w3, l1_ffn_w2):
    p = jax.nn.softmax(hgrn_lb_logits.astype(jnp.float32), axis=0)
    lower_bounds = jnp.cumsum(p, axis=0) - p[0]
    layers = [
        ((l0_mod_w, l0_mod_b, l0_w_out, l0_ffn_w1, l0_ffn_w3, l0_ffn_w2),
         (l0_w_in, l0_q_gain, l0_k_gain, l0_rpb, l0_gate_bias, l0_ml_gain)),
        ((l1_mod_w, l1_mod_b, l1_w_out, l1_ffn_w1, l1_ffn_w3, l1_ffn_w2),
         (l1_w_in, l1_hg_gain, l1_conv_w, l1_conv_b, l1_dt_bias, l1_a_log, l1_d_skip, l1_ssd_gain)),
    ]
    x_lat, x_ctx = x, ctx
    for layer in range(DEPTH):
        (mod_w, mod_b, w_out, w1, w3, w2), mix_p = layers[layer]
        sh1, sc1, g1, sh2, sc2, g2 = [m[:, None, :] for m in modulation(c, mod_w, mod_b)]
        csh1, csc1, cg1, csh2, csc2, cg2 = modulation(c_ctx, mod_w, mod_b)
        a_lat = modulate(x_lat, sh1, sc1)
        a_ctx = modulate(x_ctx, csh1, csc1)
        if layer % 2 == 0:
            y_lat, y_ctx = mixer_na_mlstm(a_lat, a_ctx, *mix_p)
        else:
            y_lat, y_ctx = mixer_hgrn_ssd(a_lat, a_ctx, lower_bounds[layer], *mix_p)
        x_lat = x_lat + g1 * (y_lat @ w_out)
        x_lat = x_lat + g2 * swiglu(modulate(x_lat, sh2, sc2), w1, w3, w2)
        if layer < DEPTH - 1:
            x_ctx = x_ctx + cg1 * (y_ctx @ w_out)
            x_ctx = x_ctx + cg2 * swiglu(modulate(x_ctx, csh2, csc2), w1, w3, w2)
    return x_lat
```

```python
import functools
import math

import jax
import jax.numpy as jnp
import numpy as np
from jax import lax
from jax.experimental import pallas as pl
from jax.experimental.pallas import tpu as pltpu

D_MODEL = 2048
BATCH = 4
SEQ = 4096
CTX_LEN = 256
GRID_W = 64
EPS = 1e-6
N_MOD = 6

NA_HEADS = 8
NA_HEAD_DIM = 128
NA_WIN_H = 8
NA_WIN_W = 16
NA_QBLK = NA_WIN_W
NA_KBLK = 2 * NA_WIN_W

ML_HEADS = 4
ML_QK_DIM = 128
ML_V_DIM = 256
ML_CHUNK = 64

HG_HEADS = 8
HG_K_DIM = 128
HG_V_DIM = 128
HG_CHUNK = 32

SSD_HEADS = 16
SSD_HEAD_DIM = 64
SSD_GROUPS = 2
SSD_STATE = 128
SSD_CONV = 5
SSD_CHUNK = 64

FFN_HIDDEN = 5632

NA_W = NA_HEADS * NA_HEAD_DIM
ML_QK_W = ML_HEADS * ML_QK_DIM
ML_W = ML_HEADS * ML_V_DIM
HG_K_W = HG_HEADS * HG_K_DIM
HG_W = HG_HEADS * HG_V_DIM
SSD_W = SSD_HEADS * SSD_HEAD_DIM
SSD_BC_W = SSD_GROUPS * SSD_STATE
SSD_XBC = SSD_W + 2 * SSD_BC_W

EVEN_SPLIT = (NA_W, NA_W, NA_W, ML_QK_W, ML_QK_W, ML_W, ML_W, 4 * ML_HEADS)
ODD_SPLIT = (HG_K_W, HG_K_W, HG_K_W, HG_W, HG_W, SSD_W, SSD_XBC, 2 * SSD_HEADS)

VMEM_LIMIT_BYTES = 56 * 1024 * 1024
LANES = 128


def _params(sem):
    return pltpu.CompilerParams(dimension_semantics=sem, vmem_limit_bytes=VMEM_LIMIT_BYTES)


def _mod_kernel(c_ref, w_ref, b_ref, o_ref):
    cv = c_ref[...]
    a = cv * jax.nn.sigmoid(cv)
    o_ref[...] = jnp.dot(a, w_ref[...], preferred_element_type=jnp.float32) + b_ref[...]


def modulation_vectors(cvecs, mod_w, mod_b, tn=1024):
    r, d = cvecs.shape
    n = mod_w.shape[1]
    return pl.pallas_call(
        _mod_kernel,
        grid=(n // tn,),
        in_specs=[pl.BlockSpec((r, d), lambda j: (0, 0)),
                  pl.BlockSpec((d, tn), lambda j: (0, j)),
                  pl.BlockSpec((1, tn), lambda j: (0, j))],
        out_specs=pl.BlockSpec((r, tn), lambda j: (0, j)),
        out_shape=jax.ShapeDtypeStruct((r, n), jnp.float32),
        compiler_params=_params(("arbitrary",)),
        name="modulation",
    )(cvecs, mod_w, mod_b.reshape(1, n))


def _inproj_kernel(x_ref, sh_ref, sc_ref, w_ref, wg_ref, o_ref, og_ref, h_ref):
    j = pl.program_id(1)

    @pl.when(j == 0)
    def _():
        x = x_ref[...]
        y = x * lax.rsqrt(jnp.mean(x * x, axis=-1, keepdims=True) + EPS)
        h = y * (1.0 + sc_ref[0]) + sh_ref[0]
        h_ref[...] = h.astype(jnp.bfloat16)
        og_ref[...] = jnp.dot(h, wg_ref[...], preferred_element_type=jnp.float32)

    o_ref[...] = jnp.dot(h_ref[...], w_ref[...], preferred_element_type=jnp.float32)


def modulated_in_proj(x, shift, scale, w_main, w_gate, rows_per_mod, tm=512, tn=512):
    m, d = x.shape
    n = w_main.shape[1]
    blocks_per_mod = rows_per_mod // tm
    mod_map = lambda i, j: (i // blocks_per_mod, 0, 0)
    return pl.pallas_call(
        _inproj_kernel,
        grid=(m // tm, n // tn),
        in_specs=[pl.BlockSpec((tm, d), lambda i, j: (i, 0)),
                  pl.BlockSpec((1, 1, d), mod_map),
                  pl.BlockSpec((1, 1, d), mod_map),
                  pl.BlockSpec((d, tn), lambda i, j: (0, j)),
                  pl.BlockSpec((d, LANES), lambda i, j: (0, 0))],
        out_specs=[pl.BlockSpec((tm, tn), lambda i, j: (i, j)),
                   pl.BlockSpec((tm, LANES), lambda i, j: (i, 0))],
        out_shape=[jax.ShapeDtypeStruct((m, n), jnp.float32),
                   jax.ShapeDtypeStruct((m, LANES), jnp.float32)],
        scratch_shapes=[pltpu.VMEM((tm, d), jnp.bfloat16)],
        compiler_params=_params(("parallel", "arbitrary")),
        name="in_proj",
    )(x, shift, scale, w_main, w_gate)


def _outproj_kernel(y_ref, x_ref, g_ref, w_ref, o_ref):
    acc = jnp.dot(y_ref[...].astype(jnp.bfloat16), w_ref[...], preferred_element_type=jnp.float32)
    o_ref[...] = x_ref[...] + g_ref[0] * acc


def out_proj_residual(y, x, gate, w_out, rows_per_mod, tm=512, tn=1024):
    m, k = y.shape
    n = w_out.shape[1]
    blocks_per_mod = rows_per_mod // tm
    return pl.pallas_call(
        _outproj_kernel,
        grid=(m // tm, n // tn),
        in_specs=[pl.BlockSpec((tm, k), lambda i, j: (i, 0)),
                  pl.BlockSpec((tm, tn), lambda i, j: (i, j)),
                  pl.BlockSpec((1, 1, tn), lambda i, j: (i // blocks_per_mod, 0, j)),
                  pl.BlockSpec((k, tn), lambda i, j: (0, j))],
        out_specs=pl.BlockSpec((tm, tn), lambda i, j: (i, j)),
        out_shape=jax.ShapeDtypeStruct((m, n), jnp.float32),
        compiler_params=_params(("parallel", "arbitrary")),
        name="out_proj",
    )(y, x, gate, w_out)


def _ffn_kernel(x_ref, sh_ref, sc_ref, g_ref, w1_ref, w3_ref, w2_ref, o_ref, h_ref, acc_ref):
    j = pl.program_id(1)

    @pl.when(j == 0)
    def _():
        x = x_ref[...]
        y = x * lax.rsqrt(jnp.mean(x * x, axis=-1, keepdims=True) + EPS)
        h_ref[...] = (y * (1.0 + sc_ref[0]) + sh_ref[0]).astype(jnp.bfloat16)
        acc_ref[...] = jnp.zeros_like(acc_ref)

    h = h_ref[...]
    a = jnp.dot(h, w1_ref[...], preferred_element_type=jnp.float32)
    b = jnp.dot(h, w3_ref[...], preferred_element_type=jnp.float32)
    u = (a * jax.nn.sigmoid(a) * b).astype(jnp.bfloat16)
    acc_ref[...] += jnp.dot(u, w2_ref[...], preferred_element_type=jnp.float32)

    @pl.when(j == pl.num_programs(1) - 1)
    def _():
        o_ref[...] = x_ref[...] + g_ref[0] * acc_ref[...]


def modulated_ffn_residual(x, shift, scale, gate, w1, w3, w2, rows_per_mod, tm=512, th=512):
    m, d = x.shape
    hid = w1.shape[1]
    blocks_per_mod = rows_per_mod // tm
    mod_map = lambda i, j: (i // blocks_per_mod, 0, 0)
    return pl.pallas_call(
        _ffn_kernel,
        grid=(m // tm, hid // th),
        in_specs=[pl.BlockSpec((tm, d), lambda i, j: (i, 0)),
                  pl.BlockSpec((1, 1, d), mod_map),
                  pl.BlockSpec((1, 1, d), mod_map),
                  pl.BlockSpec((1, 1, d), mod_map),
                  pl.BlockSpec((d, th), lambda i, j: (0, j)),
                  pl.BlockSpec((d, th), lambda i, j: (0, j)),
                  pl.BlockSpec((th, d), lambda i, j: (j, 0))],
        out_specs=pl.BlockSpec((tm, d), lambda i, j: (i, 0)),
        out_shape=jax.ShapeDtypeStruct((m, d), jnp.float32),
        scratch_shapes=[pltpu.VMEM((tm, d), jnp.bfloat16), pltpu.VMEM((tm, d), jnp.float32)],
        compiler_params=_params(("parallel", "arbitrary")),
        name="ffn",
    )(x, shift, scale, gate, w1, w3, w2)


def rms_norm(x):
    xf = x.astype(jnp.float32)
    return xf * lax.rsqrt(jnp.mean(xf * xf, axis=-1, keepdims=True) + EPS)


def split_cols(p, sizes):
    return jnp.split(p, [int(v) for v in np.cumsum(sizes)[:-1]], axis=-1)


def heads(t, n):
    b, T, _ = t.shape
    return t.reshape(b, T, n, -1).transpose(0, 2, 1, 3)


def merge_heads(t):
    b, n, T, d = t.shape
    return t.transpose(0, 2, 1, 3).reshape(b, T, n * d)


def centred_conv(x, w, b):
    k, ch = w.shape
    y = lax.conv_general_dilated(x, w[:, None, :], window_strides=(1,), padding=[(k // 2, k // 2)],
                                 dimension_numbers=('NWC', 'WIO', 'NWC'), feature_group_count=ch)
    return y + b


def chunked_scan(step, xs, state0, chunk):
    T = xs[0].shape[2]
    nc = T // chunk

    def to_chunks(a):
        a = a.reshape(a.shape[:2] + (nc, chunk) + a.shape[3:])
        return jnp.moveaxis(a, 2, 0)

    state, ys = lax.scan(step, state0, tuple(to_chunks(a) for a in xs))
    ys = jnp.moveaxis(ys, 0, 2)
    return ys.reshape(ys.shape[:2] + (T,) + ys.shape[4:]), state


def bidirectional(step, chunk, state0, ctx_fwd, lat_fwd, ctx_bwd, lat_bwd):
    yc_f, st_f = chunked_scan(step, ctx_fwd, state0, chunk)
    yl_f, _ = chunked_scan(step, lat_fwd, st_f, chunk)
    flip = lambda xs: tuple(jnp.flip(a, 2) for a in xs)
    yc_b, st_b = chunked_scan(step, flip(ctx_bwd), state0, chunk)
    yl_b, _ = chunked_scan(step, flip(lat_bwd), st_b, chunk)
    return yc_f + jnp.flip(yc_b, 2), yl_f + jnp.flip(yl_b, 2)


def mlstm_chunk(carry, xs):
    C, nrm, m = carry
    q, k, v, li, lf = xs
    L = q.shape[2]
    causal = jnp.tril(jnp.ones((L, L), bool))
    bcum = jnp.cumsum(lf, axis=-1)
    dlog = jnp.where(causal, bcum[..., :, None] - bcum[..., None, :] + li[..., None, :], -jnp.inf)
    inter = bcum + m[..., None]
    m_t = jnp.maximum(inter, jnp.max(dlog, axis=-1))
    w_inter = jnp.exp(inter - m_t)
    s = jnp.einsum('bhtk,bhsk->bhts', q, k) * jnp.exp(dlog - m_t[..., None])
    num = w_inter[..., None] * jnp.einsum('bhvk,bhtk->bhtv', C, q) + jnp.einsum('bhts,bhsv->bhtv', s, v)
    den = w_inter * jnp.einsum('bhk,bhtk->bht', nrm, q) + jnp.sum(s, axis=-1)
    h = num / jnp.maximum(jnp.abs(den), jnp.exp(-m_t))[..., None]
    b_last = bcum[..., -1]
    g = b_last[..., None] - bcum + li
    m_new = jnp.maximum(b_last + m, jnp.max(g, axis=-1))
    a = jnp.exp(b_last + m - m_new)
    w = jnp.exp(g - m_new[..., None])
    C_new = a[..., None, None] * C + jnp.einsum('bhs,bhsv,bhsk->bhvk', w, v, k)
    n_new = a[..., None] * nrm + jnp.einsum('bhs,bhsk->bhk', w, k)
    return (C_new, n_new, m_new), h


def hgrn_chunk(S, xs):
    q, k, lg, v = xs
    L = q.shape[2]
    causal = jnp.tril(jnp.ones((L, L), bool))
    G = jnp.cumsum(lg, axis=2)
    diff = G[:, :, :, None, :] - G[:, :, None, :, :]
    dec = jnp.exp(jnp.where(causal[:, :, None], diff, -jnp.inf))
    A = jnp.einsum('bhtk,bhsk,bhtsk->bhts', q, k, dec)
    o = jnp.einsum('bhts,bhsv->bhtv', A, v) + jnp.einsum('bhtk,bhkv->bhtv', q * jnp.exp(G), S)
    GL = G[:, :, -1:, :]
    S_new = jnp.exp(GL[:, :, 0])[..., None] * S + jnp.einsum('bhsk,bhsv->bhkv', k * jnp.exp(GL - G), v)
    return S_new, o


def ssd_chunk(S, xs):
    x, dt, a, Bm, Cm = xs
    nh = x.shape[1]
    L = x.shape[2]
    hpg = nh // Bm.shape[1]
    causal = jnp.tril(jnp.ones((L, L), bool))
    Bh = jnp.repeat(Bm, hpg, axis=1)
    Ch = jnp.repeat(Cm, hpg, axis=1)
    acum = jnp.cumsum(a, axis=-1)
    decay = jnp.exp(jnp.where(causal, acum[..., :, None] - acum[..., None, :], -jnp.inf))
    cb = jnp.einsum('bhtn,bhsn->bhts', Ch, Bh) * decay * dt[..., None, :]
    y = jnp.einsum('bhts,bhsp->bhtp', cb, x) + jnp.einsum('bhtn,bhpn->bhtp', Ch * jnp.exp(acum)[..., None], S)
    w = jnp.exp(acum[..., -1:] - acum) * dt
    S_new = jnp.exp(acum[..., -1])[..., None, None] * S + jnp.einsum('bhsn,bhs,bhsp->bhpn', Bh, w, x)
    return S_new, y


def neighbourhood_attention(q, k, v, qc, kc, vc, rpb):
    b, h, n, dh = q.shape
    rows = n // GRID_W
    win_h = min(NA_WIN_H, rows)
    ncb = GRID_W // NA_QBLK
    scale = dh ** -0.5
    f32 = jnp.float32
    qcol = np.arange(GRID_W).reshape(ncb, NA_QBLK)
    kstart = np.clip(np.arange(ncb) * NA_QBLK - NA_WIN_W // 2, 0, GRID_W - NA_KBLK)
    kcol = kstart[:, None] + np.arange(NA_KBLK)
    wstart = np.clip(qcol - NA_WIN_W // 2, 0, GRID_W - NA_WIN_W)
    col_ok = (kcol[:, None, :] >= wstart[..., None]) & (kcol[:, None, :] < wstart[..., None] + NA_WIN_W)
    mask = jnp.asarray(np.broadcast_to(col_ok[:, :, None, :], (ncb, NA_QBLK, win_h, NA_KBLK))
                       .reshape(ncb, NA_QBLK, win_h * NA_KBLK))
    dcol = np.clip(kcol[:, None, :] - qcol[..., None], -(NA_WIN_W - 1), NA_WIN_W - 1) + NA_WIN_W - 1
    qg = q.reshape(b, h, rows, ncb, NA_QBLK, dh)
    kg = k.reshape(b, h, rows, GRID_W, dh)
    vg = v.reshape(b, h, rows, GRID_W, dh)
    rpb32 = rpb.astype(f32)

    def row_block(r):
        r0 = jnp.clip(r - win_h // 2, 0, rows - win_h)
        qr = lax.dynamic_index_in_dim(qg, r, axis=2, keepdims=False)

        def gather_band(t):
            band = lax.dynamic_slice_in_dim(t, r0, win_h, axis=2)[:, :, :, kcol]
            return band.transpose(0, 1, 3, 2, 4, 5).reshape(b, h, ncb, win_h * NA_KBLK, dh)

        kb = gather_band(kg)
        vb = gather_band(vg)
        drow = r0 + jnp.arange(win_h) - r + NA_WIN_H - 1
        bias = rpb32[:, drow][:, :, dcol]
        bias = bias.transpose(0, 2, 3, 1, 4).reshape(h, ncb, NA_QBLK, win_h * NA_KBLK)
        s_loc = jnp.einsum('bhjqd,bhjkd->bhjqk', qr, kb).astype(f32) * scale + bias[None]
        s_loc = jnp.where(mask, s_loc, -jnp.inf)
        s_ctx = jnp.einsum('bhjqd,bhcd->bhjqc', qr, kc).astype(f32) * scale
        p = jax.nn.softmax(jnp.concatenate([s_loc, s_ctx], axis=-1), axis=-1).astype(v.dtype)
        nk = s_loc.shape[-1]
        return (jnp.einsum('bhjqk,bhjkd->bhjqd', p[..., :nk], vb)
                + jnp.einsum('bhjqc,bhcd->bhjqd', p[..., nk:], vc))

    out = lax.map(row_block, jnp.arange(rows))
    out_lat = out.transpose(1, 2, 0, 3, 4, 5).reshape(b, h, n, dh)
    s_c = jnp.einsum('bhqd,bhkd->bhqk', qc, kc).astype(f32) * scale
    out_ctx = jnp.einsum('bhqk,bhkd->bhqd', jax.nn.softmax(s_c, axis=-1).astype(vc.dtype), vc)
    return out_lat, out_ctx


def mixer_na_mlstm(lat_main, lat_gate, ctx_main, ctx_gate, q_gain, k_gain, rpb, gate_bias, ml_gain):
    f32 = jnp.float32
    lat = split_cols(lat_main, EVEN_SPLIT[:-1]) + [lat_gate[..., :4 * ML_HEADS]]
    ctx = split_cols(ctx_main, EVEN_SPLIT[:-1]) + [ctx_gate[..., :4 * ML_HEADS]]
    qkn = lambda t, g: rms_norm(heads(t, NA_HEADS)) * g
    na_lat, na_ctx = neighbourhood_attention(
        qkn(lat[0], q_gain), qkn(lat[1], k_gain), heads(lat[2], NA_HEADS),
        qkn(ctx[0], q_gain), qkn(ctx[1], k_gain), heads(ctx[2], NA_HEADS), rpb)

    def ml_inputs(p):
        q = heads(p[3], ML_HEADS).astype(f32) * ML_QK_DIM ** -0.5
        k = heads(p[4], ML_HEADS).astype(f32)
        v = heads(p[5], ML_HEADS).astype(f32)
        g = (p[7] + gate_bias.reshape(-1)).astype(f32)
        bb, T, _ = g.shape
        g = g.reshape(bb, T, 4, ML_HEADS).transpose(2, 0, 3, 1)
        fwd = (q, k, v, g[0], jax.nn.log_sigmoid(g[1]))
        bwd = (q, k, v, g[2], jax.nn.log_sigmoid(g[3]))
        return fwd, bwd

    c_fwd, c_bwd = ml_inputs(ctx)
    l_fwd, l_bwd = ml_inputs(lat)
    b = lat_main.shape[0]
    state0 = (jnp.zeros((b, ML_HEADS, ML_V_DIM, ML_QK_DIM), f32),
              jnp.zeros((b, ML_HEADS, ML_QK_DIM), f32),
              jnp.zeros((b, ML_HEADS), f32))
    ml_ctx, ml_lat = bidirectional(mlstm_chunk, ML_CHUNK, state0, c_fwd, l_fwd, c_bwd, l_bwd)
    ml_out = lambda hs, p: merge_heads(rms_norm(hs)) * ml_gain * jax.nn.sigmoid(p[6])
    y_lat = jnp.concatenate([merge_heads(na_lat), ml_out(ml_lat, lat)], axis=-1)
    y_ctx = jnp.concatenate([merge_heads(na_ctx), ml_out(ml_ctx, ctx)], axis=-1)
    return y_lat, y_ctx


def mixer_hgrn_ssd(lat_main, lat_gate, ctx_main, ctx_gate, lower_bound, hg_gain, conv_w, conv_b, dt_bias, a_log,
                   d_skip, ssd_gain):
    f32 = jnp.float32
    lat = split_cols(lat_main, ODD_SPLIT[:-1]) + [lat_gate[..., :2 * SSD_HEADS]]
    ctx = split_cols(ctx_main, ODD_SPLIT[:-1]) + [ctx_gate[..., :2 * SSD_HEADS]]
    b = lat_main.shape[0]
    lbound = lower_bound.reshape(HG_HEADS, 1, HG_K_DIM).astype(f32)

    def hg_inputs(p):
        q = jax.nn.silu(heads(p[0], HG_HEADS).astype(f32))
        v = jax.nn.silu(heads(p[3], HG_HEADS).astype(f32))

        def one_dir(fp):
            f = lbound + (1 - lbound) * jax.nn.sigmoid(heads(fp, HG_HEADS).astype(f32))
            return (q, 1 - f, jnp.log(f), v)

        return one_dir(p[1]), one_dir(p[2])

    c_fwd, c_bwd = hg_inputs(ctx)
    l_fwd, l_bwd = hg_inputs(lat)
    hg0 = jnp.zeros((b, HG_HEADS, HG_K_DIM, HG_V_DIM), f32)
    hg_ctx, hg_lat = bidirectional(hgrn_chunk, HG_CHUNK, hg0, c_fwd, l_fwd, c_bwd, l_bwd)
    hg_out = lambda o, p: merge_heads(rms_norm(o)) * hg_gain * jax.nn.silu(p[4])
    A = -jnp.exp(a_log.astype(f32))

    def ssd_inputs(p):
        xbc = jax.nn.silu(centred_conv(p[6], conv_w, conv_b))
        xs, bm, cm = split_cols(xbc, (SSD_W, SSD_BC_W, SSD_BC_W))
        xh = heads(xs, SSD_HEADS).astype(f32)
        bm = heads(bm, SSD_GROUPS).astype(f32)
        cm = heads(cm, SSD_GROUPS).astype(f32)
        bb, T, _ = p[7].shape
        dts = jax.nn.softplus(p[7].astype(f32).reshape(bb, T, 2, SSD_HEADS) + dt_bias.astype(f32))
        dts = dts.transpose(2, 0, 3, 1)
        fwd = (xh, dts[0], dts[0] * A[0][:, None], bm, cm)
        bwd = (xh, dts[1], dts[1] * A[1][:, None], bm, cm)
        return fwd, bwd, xh

    sc_fwd, sc_bwd, xh_ctx = ssd_inputs(ctx)
    sl_fwd, sl_bwd, xh_lat = ssd_inputs(lat)
    ssd0 = jnp.zeros((b, SSD_HEADS, SSD_HEAD_DIM, SSD_STATE), f32)
    ys_ctx, ys_lat = bidirectional(ssd_chunk, SSD_CHUNK, ssd0, sc_fwd, sl_fwd, sc_bwd, sl_bwd)

    def ssd_out(ysum, xh, z):
        y = merge_heads(ysum + d_skip.astype(f32)[:, None, None] * xh) * jax.nn.silu(z.astype(f32))
        bb, T, _ = y.shape
        y = rms_norm(y.reshape(bb, T, SSD_GROUPS, -1)).reshape(bb, T, SSD_W)
        return y * ssd_gain

    y_lat = jnp.concatenate([hg_out(hg_lat, lat), ssd_out(ys_lat, xh_lat, lat[5])], axis=-1)
    y_ctx = jnp.concatenate([hg_out(hg_ctx, ctx), ssd_out(ys_ctx, xh_ctx, ctx[5])], axis=-1)
    return y_lat, y_ctx


def _split_w_in(w_in, n_gate):
    n_main = w_in.shape[1] - n_gate
    w_gate = jnp.zeros((w_in.shape[0], LANES), jnp.float32).at[:, :n_gate].set(w_in[:, n_main:])
    return w_in[:, :n_main].astype(jnp.bfloat16), w_gate


def kernel(x, c, ctx, c_ctx, hgrn_lb_logits,
           l0_mod_w, l0_mod_b, l0_w_in, l0_q_gain, l0_k_gain, l0_rpb, l0_gate_bias, l0_ml_gain,
           l0_w_out, l0_ffn_w1, l0_ffn_w3, l0_ffn_w2,
           l1_mod_w, l1_mod_b, l1_w_in, l1_hg_gain, l1_conv_w, l1_conv_b, l1_dt_bias, l1_a_log,
           l1_d_skip, l1_ssd_gain, l1_w_out, l1_ffn_w1, l1_ffn_w3, l1_ffn_w2):
    f32 = jnp.float32
    bf16 = jnp.bfloat16
    b, t, d = x.shape
    tc = ctx.shape[1]
    p = jax.nn.softmax(hgrn_lb_logits.astype(f32), axis=0)
    lower_bounds = jnp.cumsum(p, axis=0) - p[0]

    cvecs = jnp.zeros((8, d), f32).at[:b].set(c).at[b].set(c_ctx)
    x_lat = x.reshape(b * t, d)
    x_ctx = ctx.reshape(b * tc, d)

    layers = [
        (l0_mod_w, l0_mod_b, l0_w_in, 4 * ML_HEADS, l0_w_out, l0_ffn_w1, l0_ffn_w3, l0_ffn_w2),
        (l1_mod_w, l1_mod_b, l1_w_in, 2 * SSD_HEADS, l1_w_out, l1_ffn_w1, l1_ffn_w3, l1_ffn_w2),
    ]
    for layer, (mod_w, mod_b, w_in, n_gate, w_out, w1, w3, w2) in enumerate(layers):
        mods = modulation_vectors(cvecs, mod_w, mod_b).reshape(8, N_MOD, 1, d)
        lat_mod = [mods[:b, i] for i in range(N_MOD)]
        ctx_mod = [mods[b:b + 1, i] for i in range(N_MOD)]
        w_main, w_gate = _split_w_in(w_in, n_gate)
        lat_main, lat_gate = modulated_in_proj(x_lat, lat_mod[0], lat_mod[1], w_main, w_gate, t)
        ctx_main, ctx_gate = modulated_in_proj(x_ctx, ctx_mod[0], ctx_mod[1], w_main, w_gate, b * tc)
        r3 = lambda a, tt: a.reshape(b, tt, a.shape[-1])
        if layer == 0:
            y_lat, y_ctx = mixer_na_mlstm(r3(lat_main, t), r3(lat_gate, t), r3(ctx_main, tc), r3(ctx_gate, tc),
                                          l0_q_gain, l0_k_gain, l0_rpb, l0_gate_bias, l0_ml_gain)
        else:
            y_lat, y_ctx = mixer_hgrn_ssd(r3(lat_main, t), r3(lat_gate, t), r3(ctx_main, tc), r3(ctx_gate, tc),
                                          lower_bounds[layer], l1_hg_gain, l1_conv_w, l1_conv_b, l1_dt_bias,
                                          l1_a_log, l1_d_skip, l1_ssd_gain)
        w_out_b = w_out.astype(bf16)
        w1_b, w3_b, w2_b = w1.astype(bf16), w3.astype(bf16), w2.astype(bf16)
        x_lat = out_proj_residual(y_lat.reshape(b * t, -1), x_lat, lat_mod[2], w_out_b, t)
        x_lat = modulated_ffn_residual(x_lat, lat_mod[3], lat_mod[4], lat_mod[5], w1_b, w3_b, w2_b, t)
        if layer == 0:
            x_ctx = out_proj_residual(y_ctx.reshape(b * tc, -1), x_ctx, ctx_mod[2], w_out_b, b * tc)
            x_ctx = modulated_ffn_residual(x_ctx, ctx_mod[3], ctx_mod[4], ctx_mod[5], w1_b, w3_b, w2_b, b * tc)
    return x_lat.reshape(b, t, d)
```

```python
import functools

import jax
import jax.numpy as jnp
import numpy as np
from jax import lax
from jax.experimental import pallas as pl
from jax.experimental.pallas import tpu as pltpu

EPS = 1e-6
N_MOD = 6
GRID_W = 64

NA_HEADS = 8
NA_HEAD_DIM = 128
NA_WIN_H = 8
NA_WIN_W = 16
NA_GROUP_ROWS = 4
NA_BAND_ROWS = NA_GROUP_ROWS + NA_WIN_H - 1
NA_GROUP = NA_GROUP_ROWS * GRID_W
NA_BAND = NA_BAND_ROWS * GRID_W

ML_HEADS = 4
ML_QK_DIM = 128
ML_V_DIM = 256
ML_CHUNK = 64

HG_HEADS = 8
HG_K_DIM = 128
HG_V_DIM = 128
HG_CHUNK = 32
HG_SUB = 8

SSD_HEADS = 16
SSD_HEAD_DIM = 64
SSD_GROUPS = 2
SSD_STATE = 128
SSD_CONV = 5
SSD_CHUNK = 64

NA_W = NA_HEADS * NA_HEAD_DIM
ML_QK_W = ML_HEADS * ML_QK_DIM
ML_W = ML_HEADS * ML_V_DIM
HG_K_W = HG_HEADS * HG_K_DIM
HG_W = HG_HEADS * HG_V_DIM
SSD_W = SSD_HEADS * SSD_HEAD_DIM
SSD_BC_W = SSD_GROUPS * SSD_STATE
SSD_XBC = SSD_W + 2 * SSD_BC_W

EVEN_Q, EVEN_K, EVEN_V = 0, NA_W, 2 * NA_W
EVEN_MQ = 3 * NA_W
EVEN_MK = EVEN_MQ + ML_QK_W
EVEN_MV = EVEN_MK + ML_QK_W
EVEN_MO = EVEN_MV + ML_W
ODD_Q, ODD_FF, ODD_FB = 0, HG_K_W, 2 * HG_K_W
ODD_I = 3 * HG_K_W
ODD_G = ODD_I + HG_W
ODD_Z = ODD_G + HG_W
ODD_XBC = ODD_Z + SSD_W

VMEM_LIMIT_BYTES = 56 * 1024 * 1024
LANES = 128
GATE_ROWS = 32
NEG = -1e30

F32 = jnp.float32
BF16 = jnp.bfloat16
HIGHEST = lax.Precision.HIGHEST


def _params(sem):
    return pltpu.CompilerParams(dimension_semantics=sem, vmem_limit_bytes=VMEM_LIMIT_BYTES)


def _dot(a, b):
    return jnp.dot(a.astype(BF16), b.astype(BF16), preferred_element_type=F32)


def _dot_nt(a, b):
    return lax.dot_general(a.astype(BF16), b.astype(BF16), (((1,), (1,)), ((), ())), preferred_element_type=F32)


def _dot_tn(a, b):
    return lax.dot_general(a.astype(BF16), b.astype(BF16), (((0,), (0,)), ((), ())), preferred_element_type=F32)


def _dot_f32(a, b):
    return jnp.dot(a, b, precision=HIGHEST, preferred_element_type=F32)


def _dot_nt_f32(a, b):
    return lax.dot_general(a, b, (((1,), (1,)), ((), ())), precision=HIGHEST, preferred_element_type=F32)


def _sigmoid(x):
    return 1.0 / (1.0 + jnp.exp(-x))


def _silu(x):
    return x * _sigmoid(x)


def _log_sigmoid(x):
    return jnp.minimum(x, 0.0) - jnp.log(1.0 + jnp.exp(-jnp.abs(x)))


def _softplus(x):
    return jnp.maximum(x, 0.0) + jnp.log(1.0 + jnp.exp(-jnp.abs(x)))


def _rms(x):
    return x * lax.rsqrt(jnp.mean(x * x, axis=-1, keepdims=True) + EPS)


def _time_tri(n, rev):
    t = lax.broadcasted_iota(jnp.int32, (n, n), 0)
    u = lax.broadcasted_iota(jnp.int32, (n, n), 1)
    return (u >= t) if rev else (u <= t)


def _chunk_index(step, n_lat, n_ctx, rev):
    if rev:
        return n_lat + n_ctx - 1 - step
    return jnp.where(step < n_ctx, n_lat + step, step - n_ctx)


def _mod_kernel(c_ref, w_ref, b_ref, o_ref):
    o_ref[...] = jnp.dot(_silu(c_ref[...]), w_ref[...], preferred_element_type=F32) + b_ref[...]


def modulation_vectors(cvecs, mod_w, mod_b, tn=1024):
    r, d = cvecs.shape
    n = mod_w.shape[1]
    return pl.pallas_call(
        _mod_kernel,
        grid=(n // tn,),
        in_specs=[pl.BlockSpec((r, d), lambda j: (0, 0)),
                  pl.BlockSpec((d, tn), lambda j: (0, j)),
                  pl.BlockSpec((1, tn), lambda j: (0, j))],
        out_specs=pl.BlockSpec((r, tn), lambda j: (0, j)),
        out_shape=jax.ShapeDtypeStruct((r, n), F32),
        compiler_params=_params(("arbitrary",)),
        name="modulation",
    )(cvecs, mod_w, mod_b.reshape(1, n))


def _mod_map(n_lat_blocks, n_batch):
    return lambda b, i, j: (jnp.where(i >= n_lat_blocks, n_batch, b), 0, 0)


def _inproj_kernel(x_ref, sh_ref, sc_ref, w_ref, wg_ref, o_ref, og_ref, h_ref):
    @pl.when(pl.program_id(2) == 0)
    def _():
        h = _rms(x_ref[0]) * (1.0 + sc_ref[0]) + sh_ref[0]
        h_ref[...] = h.astype(BF16)
        og_ref[0] = jnp.dot(h, wg_ref[...], preferred_element_type=F32)

    o_ref[0] = jnp.dot(h_ref[...], w_ref[...], preferred_element_type=F32)


def modulated_in_proj(x, shift, scale, w_main, w_gate, t_lat, tm=512, tn=512):
    nb, t, d = x.shape
    n = w_main.shape[1]
    mod_map = _mod_map(t_lat // tm, nb)
    return pl.pallas_call(
        _inproj_kernel,
        grid=(nb, pl.cdiv(t, tm), n // tn),
        in_specs=[pl.BlockSpec((1, tm, d), lambda b, i, j: (b, i, 0)),
                  pl.BlockSpec((1, 1, d), mod_map),
                  pl.BlockSpec((1, 1, d), mod_map),
                  pl.BlockSpec((d, tn), lambda b, i, j: (0, j)),
                  pl.BlockSpec((d, LANES), lambda b, i, j: (0, 0))],
        out_specs=[pl.BlockSpec((1, tm, tn), lambda b, i, j: (b, i, j)),
                   pl.BlockSpec((1, tm, LANES), lambda b, i, j: (b, i, 0))],
        out_shape=[jax.ShapeDtypeStruct((nb, t, n), F32),
                   jax.ShapeDtypeStruct((nb, t, LANES), F32)],
        scratch_shapes=[pltpu.VMEM((tm, d), BF16)],
        compiler_params=_params(("parallel", "parallel", "arbitrary")),
        name="in_proj",
    )(x, shift, scale, w_main, w_gate)


def _outproj_kernel(ya_ref, yb_ref, x_ref, g_ref, wa_ref, wb_ref, o_ref):
    acc = jnp.dot(ya_ref[0], wa_ref[...], preferred_element_type=F32)
    acc += jnp.dot(yb_ref[0], wb_ref[...], preferred_element_type=F32)
    o_ref[0] = x_ref[0] + g_ref[0] * acc


def out_proj_residual(ya, yb, x, gate, w_a, w_b, t_lat, t_out, tm=512, tn=1024):
    nb, _, k = ya.shape
    n = w_a.shape[1]
    mod_map = _mod_map(t_lat // tm, nb)
    return pl.pallas_call(
        _outproj_kernel,
        grid=(nb, pl.cdiv(t_out, tm), n // tn),
        in_specs=[pl.BlockSpec((1, tm, k), lambda b, i, j: (b, i, 0)),
                  pl.BlockSpec((1, tm, k), lambda b, i, j: (b, i, 0)),
                  pl.BlockSpec((1, tm, tn), lambda b, i, j: (b, i, j)),
                  pl.BlockSpec((1, 1, tn), lambda b, i, j: mod_map(b, i, j)[:2] + (j,)),
                  pl.BlockSpec((k, tn), lambda b, i, j: (0, j)),
                  pl.BlockSpec((k, tn), lambda b, i, j: (0, j))],
        out_specs=pl.BlockSpec((1, tm, tn), lambda b, i, j: (b, i, j)),
        out_shape=jax.ShapeDtypeStruct((nb, t_out, n), F32),
        compiler_params=_params(("parallel", "parallel", "arbitrary")),
        name="out_proj",
    )(ya, yb, x, gate, w_a, w_b)


def _ffn_kernel(x_ref, sh_ref, sc_ref, g_ref, w1_ref, w3_ref, w2_ref, o_ref, h_ref, acc_ref):
    j = pl.program_id(2)

    @pl.when(j == 0)
    def _():
        h_ref[...] = (_rms(x_ref[0]) * (1.0 + sc_ref[0]) + sh_ref[0]).astype(BF16)
        acc_ref[...] = jnp.zeros_like(acc_ref)

    h = h_ref[...]
    a = jnp.dot(h, w1_ref[...], preferred_element_type=F32)
    b = jnp.dot(h, w3_ref[...], preferred_element_type=F32)
    acc_ref[...] += jnp.dot((_silu(a) * b).astype(BF16), w2_ref[...], preferred_element_type=F32)

    @pl.when(j == pl.num_programs(2) - 1)
    def _():
        o_ref[0] = x_ref[0] + g_ref[0] * acc_ref[...]


def modulated_ffn_residual(x, shift, scale, gate, w1, w3, w2, t_lat, tm=512, th=512):
    nb, t, d = x.shape
    hid = w1.shape[1]
    mod_map = _mod_map(t_lat // tm, nb)
    return pl.pallas_call(
        _ffn_kernel,
        grid=(nb, pl.cdiv(t, tm), hid // th),
        in_specs=[pl.BlockSpec((1, tm, d), lambda b, i, j: (b, i, 0)),
                  pl.BlockSpec((1, 1, d), mod_map),
                  pl.BlockSpec((1, 1, d), mod_map),
                  pl.BlockSpec((1, 1, d), mod_map),
                  pl.BlockSpec((d, th), lambda b, i, j: (0, j)),
                  pl.BlockSpec((d, th), lambda b, i, j: (0, j)),
                  pl.BlockSpec((th, d), lambda b, i, j: (j, 0))],
        out_specs=pl.BlockSpec((1, tm, d), lambda b, i, j: (b, i, 0)),
        out_shape=jax.ShapeDtypeStruct((nb, t, d), F32),
        scratch_shapes=[pltpu.VMEM((tm, d), BF16), pltpu.VMEM((tm, d), F32)],
        compiler_params=_params(("parallel", "parallel", "arbitrary")),
        name="ffn",
    )(x, shift, scale, gate, w1, w3, w2)


def _na_geometry(rows):
    n_groups = rows // NA_GROUP_ROWS
    band0 = np.clip(np.arange(n_groups) * NA_GROUP_ROWS - NA_WIN_H // 2, 0, rows - NA_BAND_ROWS)
    return n_groups, band0


def na_bias_tables(rpb, rows):
    n_groups, band0 = _na_geometry(rows)
    tables = []
    for g in (0, 1, n_groups - 1):
        qr = g * NA_GROUP_ROWS + np.arange(NA_GROUP_ROWS)
        r0 = np.clip(qr - NA_WIN_H // 2, 0, rows - NA_WIN_H)
        kr = band0[g] + np.arange(NA_BAND_ROWS)
        qc = np.arange(GRID_W)
        kc = np.arange(GRID_W)
        w0 = np.clip(qc - NA_WIN_W // 2, 0, GRID_W - NA_WIN_W)
        row_ok = (kr[None, :] >= r0[:, None]) & (kr[None, :] < r0[:, None] + NA_WIN_H)
        col_ok = (kc[None, :] >= w0[:, None]) & (kc[None, :] < w0[:, None] + NA_WIN_W)
        drow = np.clip(kr[None, :] - qr[:, None] + NA_WIN_H - 1, 0, 2 * NA_WIN_H - 2)
        dcol = np.clip(kc[None, :] - qc[:, None], -(NA_WIN_W - 1), NA_WIN_W - 1) + NA_WIN_W - 1
        ok = row_ok[:, None, :, None] & col_ok[None, :, None, :]
        di = np.broadcast_to(drow[:, None, :, None], ok.shape).reshape(NA_GROUP, NA_BAND)
        dj = np.broadcast_to(dcol[None, :, None, :], ok.shape).reshape(NA_GROUP, NA_BAND)
        bias = rpb.astype(F32)[:, di, dj]
        tables.append(jnp.where(jnp.asarray(ok.reshape(NA_GROUP, NA_BAND)), bias, NEG))
    return jnp.stack(tables)


def _na_kernel(q_ref, k_ref, v_ref, bias_ref, qg_ref, kg_ref, o_ref, *, rows, t_lat):
    g = pl.program_id(2)
    n_groups = rows // NA_GROUP_ROWS
    qn = (_rms(q_ref[0]) * qg_ref[...] * NA_HEAD_DIM ** -0.5).astype(BF16)
    kc = (_rms(k_ref[0, t_lat:, :]) * kg_ref[...]).astype(BF16)
    vc = v_ref[0, t_lat:, :].astype(BF16)
    s_ctx = _dot_nt(qn, kc)

    @pl.when(g < n_groups)
    def _():
        band0 = jnp.clip(g * NA_GROUP_ROWS - NA_WIN_H // 2, 0, rows - NA_BAND_ROWS)
        start = pl.multiple_of(band0 * GRID_W, GRID_W)
        kb = (_rms(k_ref[0, pl.ds(start, NA_BAND), :]) * kg_ref[...]).astype(BF16)
        vb = v_ref[0, pl.ds(start, NA_BAND), :].astype(BF16)
        s_loc = _dot_nt(qn, kb) + bias_ref[0, 0]
        m = jnp.maximum(jnp.max(s_loc, axis=-1, keepdims=True), jnp.max(s_ctx, axis=-1, keepdims=True))
        p_loc = jnp.exp(s_loc - m)
        p_ctx = jnp.exp(s_ctx - m)
        den = jnp.sum(p_loc, axis=-1, keepdims=True) + jnp.sum(p_ctx, axis=-1, keepdims=True)
        o_ref[0] = ((_dot(p_loc, vb) + _dot(p_ctx, vc)) / den).astype(o_ref.dtype)

    @pl.when(g >= n_groups)
    def _():
        p = jnp.exp(s_ctx - jnp.max(s_ctx, axis=-1, keepdims=True))
        o_ref[0] = (_dot(p, vc) / jnp.sum(p, axis=-1, keepdims=True)).astype(o_ref.dtype)


def neighbourhood_attention(proj, bias, q_gain, k_gain, t_lat):
    nb, t, _ = proj.shape
    rows = t_lat // GRID_W
    n_groups, _ = _na_geometry(rows)
    assert t - t_lat == NA_GROUP
    table = lambda b, h, g: (jnp.where(g == 0, 0, jnp.where(g >= n_groups - 1, 2, 1)), h, 0, 0)
    return pl.pallas_call(
        functools.partial(_na_kernel, rows=rows, t_lat=t_lat),
        grid=(nb, NA_HEADS, n_groups + 1),
        in_specs=[pl.BlockSpec((1, NA_GROUP, NA_HEAD_DIM), lambda b, h, g: (b, g, EVEN_Q // NA_HEAD_DIM + h)),
                  pl.BlockSpec((1, t, NA_HEAD_DIM), lambda b, h, g: (b, 0, EVEN_K // NA_HEAD_DIM + h)),
                  pl.BlockSpec((1, t, NA_HEAD_DIM), lambda b, h, g: (b, 0, EVEN_V // NA_HEAD_DIM + h)),
                  pl.BlockSpec((1, 1, NA_GROUP, NA_BAND), table),
                  pl.BlockSpec((1, NA_HEAD_DIM), lambda b, h, g: (0, 0)),
                  pl.BlockSpec((1, NA_HEAD_DIM), lambda b, h, g: (0, 0))],
        out_specs=pl.BlockSpec((1, NA_GROUP, NA_HEAD_DIM), lambda b, h, g: (b, g, h)),
        out_shape=jax.ShapeDtypeStruct((nb, t, NA_W), BF16),
        compiler_params=_params(("parallel", "parallel", "arbitrary")),
        name="natten",
    )(proj, proj, proj, bias, q_gain.reshape(1, -1), k_gain.reshape(1, -1))


def _mlstm_kernel(*refs, rev):
    if rev:
        (q_ref, k_ref, v_ref, gc_ref, gr_ref, gbr_ref, gbc_ref, prev_ref, og_ref, gain_ref,
         o_ref, c_ref, n_ref, m_ref) = refs
    else:
        q_ref, k_ref, v_ref, gc_ref, gr_ref, gbr_ref, gbc_ref, o_ref, c_ref, n_ref, m_ref = refs
    L = ML_CHUNK

    @pl.when(pl.program_id(1) == 0)
    def _():
        c_ref[...] = jnp.zeros_like(c_ref)
        n_ref[...] = jnp.zeros_like(n_ref)
        m_ref[...] = jnp.zeros_like(m_ref)

    base = 2 * ML_HEADS if rev else 0
    gcol = gc_ref[0][:, :4 * ML_HEADS] + gbr_ref[:, :4 * ML_HEADS]
    grow = gr_ref[0, 0][:4 * ML_HEADS, :] + gbc_ref[:4 * ML_HEADS, :]
    li_col = gcol[:, base:base + ML_HEADS]
    lf_col = _log_sigmoid(gcol[:, base + ML_HEADS:base + 2 * ML_HEADS])
    li_row = grow[base:base + ML_HEADS, :]
    lf_row = _log_sigmoid(grow[base + ML_HEADS:base + 2 * ML_HEADS, :])
    mask = _time_tri(L, rev)
    tri = mask.astype(F32)
    bc_col = _dot_f32(tri, lf_col)
    bc_row = _dot_nt_f32(lf_row, tri)
    last = 0 if rev else L - 1

    for h in range(ML_HEADS):
        bcol, brow = bc_col[:, h:h + 1], bc_row[h:h + 1, :]
        licol, lirow = li_col[:, h:h + 1], li_row[h:h + 1, :]
        m_prev = m_ref[h]
        q = q_ref[0][:, h * ML_QK_DIM:(h + 1) * ML_QK_DIM] * ML_QK_DIM ** -0.5
        k = k_ref[0][:, h * ML_QK_DIM:(h + 1) * ML_QK_DIM]
        v = v_ref[0][:, h * ML_V_DIM:(h + 1) * ML_V_DIM]
        dlog = jnp.where(mask, bcol - brow + lirow, NEG)
        inter = bcol + m_prev
        m_t = jnp.maximum(inter, jnp.max(dlog, axis=-1, keepdims=True))
        w_inter = jnp.exp(inter - m_t)
        s = _dot_nt(q, k) * jnp.exp(dlog - m_t)
        c_state, n_state = c_ref[h], n_ref[h]
        num = w_inter * _dot_nt(q, c_state) + _dot(s, v)
        den = w_inter * jnp.sum(q * n_state, axis=-1, keepdims=True) + jnp.sum(s, axis=-1, keepdims=True)
        hid = num / jnp.maximum(jnp.abs(den), jnp.exp(-m_t))
        b_last = bcol[last:last + 1, :]
        gg = b_last - bcol + licol
        m_new = jnp.maximum(b_last + m_prev, jnp.max(gg, axis=0, keepdims=True))
        decay = jnp.exp(b_last + m_prev - m_new)
        w = jnp.exp(gg - m_new)
        c_ref[h] = decay * c_state + _dot_tn(v * w, k)
        n_ref[h] = decay * n_state + jnp.sum(k * w, axis=0, keepdims=True)
        m_ref[h] = m_new
        cols = slice(h * ML_V_DIM, (h + 1) * ML_V_DIM)
        if rev:
            tot = hid + prev_ref[0][:, cols]
            y = _rms(tot) * gain_ref[:, cols] * _sigmoid(og_ref[0][:, cols])
            o_ref[0, :, cols] = y.astype(o_ref.dtype)
        else:
            o_ref[0, :, cols] = hid


def _gate_rows(gates, chunk):
    nb, t, _ = gates.shape
    return gates[..., :GATE_ROWS].reshape(nb, t // chunk, chunk, GATE_ROWS).transpose(0, 1, 3, 2)


def _lane_row(v):
    return jnp.zeros((1, LANES), F32).at[0, :v.size].set(v.reshape(-1).astype(F32))


def _sublane_col(v):
    return jnp.zeros((GATE_ROWS, 1), F32).at[:v.size, 0].set(v.reshape(-1).astype(F32))


def mlstm(proj, gates, gate_bias, ml_gain, t_lat):
    nb, t, _ = proj.shape
    L = ML_CHUNK
    n_lat, n_ctx = t_lat // L, (t - t_lat) // L
    gate_rows = _gate_rows(gates, L)
    gb_row, gb_col = _lane_row(gate_bias), _sublane_col(gate_bias)

    def call(rev, prev):
        idx = lambda b, s: _chunk_index(s, n_lat, n_ctx, rev)
        tok = lambda width, col: pl.BlockSpec((1, L, width), lambda b, s: (b, idx(b, s), col // width))
        in_specs = [tok(ML_QK_W, EVEN_MQ), tok(ML_QK_W, EVEN_MK), tok(ML_W, EVEN_MV),
                    pl.BlockSpec((1, L, LANES), lambda b, s: (b, idx(b, s), 0)),
                    pl.BlockSpec((1, 1, GATE_ROWS, L), lambda b, s: (b, idx(b, s), 0, 0)),
                    pl.BlockSpec((1, LANES), lambda b, s: (0, 0)),
                    pl.BlockSpec((GATE_ROWS, 1), lambda b, s: (0, 0))]
        args = [proj, proj, proj, gates, gate_rows, gb_row, gb_col]
        if rev:
            in_specs += [tok(ML_W, 0), tok(ML_W, EVEN_MO), pl.BlockSpec((1, ML_W), lambda b, s: (0, 0))]
            args += [prev, proj, ml_gain.reshape(1, -1)]
        return pl.pallas_call(
            functools.partial(_mlstm_kernel, rev=rev),
            grid=(nb, n_lat + n_ctx),
            in_specs=in_specs,
            out_specs=tok(ML_W, 0),
            out_shape=jax.ShapeDtypeStruct((nb, t, ML_W), BF16 if rev else F32),
            scratch_shapes=[pltpu.VMEM((ML_HEADS, ML_V_DIM, ML_QK_DIM), F32),
                            pltpu.VMEM((ML_HEADS, 1, ML_QK_DIM), F32),
                            pltpu.VMEM((ML_HEADS, 1, 1), F32)],
            compiler_params=_params(("parallel", "arbitrary")),
            name="mlstm_bwd" if rev else "mlstm_fwd",
        )(*args)

    return call(True, call(False, None))


def _hgrn_kernel(*refs, rev, heads_per_step):
    if rev:
        q_ref, f_ref, i_ref, lb_ref, prev_ref, og_ref, gain_ref, o_ref, s_ref, p_ref, od_ref = refs
    else:
        q_ref, f_ref, i_ref, lb_ref, o_ref, s_ref, p_ref, od_ref = refs
    L, SUB = HG_CHUNK, HG_SUB
    half = L // 2

    @pl.when(pl.program_id(2) == 0)
    def _():
        s_ref[...] = jnp.zeros_like(s_ref)

    mask = _time_tri(L, rev)
    tri = mask.astype(F32)
    t_id = lax.broadcasted_iota(jnp.int32, (L, L), 0)
    u_id = lax.broadcasted_iota(jnp.int32, (L, L), 1)
    row_id = lax.broadcasted_iota(jnp.int32, (L, 1), 0)
    sub_id = lax.broadcasted_iota(jnp.int32, (SUB, 1), 0)
    same_half = (t_id // half) == (u_id // half)
    if rev:
        m1 = (t_id < half) & (u_id >= half)
        m2 = same_half & (t_id % half < SUB) & (u_id % half >= SUB)
        r1, r2a, r2b, last = half, SUB, half + SUB, 0
    else:
        m1 = (t_id >= half) & (u_id < half)
        m2 = same_half & (t_id % half >= SUB) & (u_id % half < SUB)
        r1, r2a, r2b, last = half - 1, SUB - 1, half + SUB - 1, L - 1
    ones = jnp.ones((HG_K_DIM, LANES), BF16)

    for h in range(heads_per_step):
        cols = slice(h * HG_K_DIM, (h + 1) * HG_K_DIM)
        lb = lb_ref[:, cols]
        q = _silu(q_ref[0][:, cols])
        v = _silu(i_ref[0][:, cols])
        f = lb + (1.0 - lb) * _sigmoid(f_ref[0][:, cols])
        kk = 1.0 - f
        G = _dot_f32(tri, jnp.log(f))
        a1 = G[r1:r1 + 1, :]
        a2 = jnp.where(row_id < half, G[r2a:r2a + 1, :], G[r2b:r2b + 1, :])
        A1 = _dot_nt(q * jnp.exp(jnp.minimum(G - a1, 0.0)), kk * jnp.exp(jnp.minimum(a1 - G, 0.0)))
        A2 = _dot_nt(q * jnp.exp(jnp.minimum(G - a2, 0.0)), kk * jnp.exp(jnp.minimum(a2 - G, 0.0)))
        a_off = jnp.where(m1, A1, jnp.where(m2, A2, 0.0))
        for blk in range(L // SUB):
            kb = kk[blk * SUB:(blk + 1) * SUB, :]
            gb = G[blk * SUB:(blk + 1) * SUB, :]
            for t in range(SUB):
                r = blk * SUB + t
                ok = (sub_id >= t) if rev else (sub_id <= t)
                dec = jnp.exp(jnp.where(ok, G[r:r + 1, :] - gb, NEG))
                p_ref[r * SUB:(r + 1) * SUB, :] = q[r:r + 1, :] * kb * dec
        rsum = _dot(p_ref[...], ones)
        for blk in range(L // SUB):
            vb = v[blk * SUB:(blk + 1) * SUB, :]
            for t in range(SUB):
                r = blk * SUB + t
                od_ref[r:r + 1, :] = jnp.sum(rsum[r * SUB:(r + 1) * SUB, :] * vb, axis=0, keepdims=True)
        st = s_ref[h]
        o = _dot(a_off, v) + od_ref[...] + _dot_nt(q * jnp.exp(G), st)
        gl = G[last:last + 1, :]
        s_ref[h] = st * jnp.exp(gl) + _dot_tn(v, kk * jnp.exp(gl - G))
        if rev:
            tot = o + prev_ref[0][:, cols]
            y = _rms(tot) * gain_ref[:, cols] * _silu(og_ref[0][:, cols])
            o_ref[0, :, cols] = y.astype(o_ref.dtype)
        else:
            o_ref[0, :, cols] = o


def hgrn2(proj, lower_bound, hg_gain, t_lat, heads_per_step=2):
    nb, t, _ = proj.shape
    L = HG_CHUNK
    n_lat, n_ctx = t_lat // L, (t - t_lat) // L
    width = heads_per_step * HG_K_DIM
    n_hb = HG_HEADS // heads_per_step

    def call(rev, prev):
        idx = lambda s: _chunk_index(s, n_lat, n_ctx, rev)
        tok = lambda col: pl.BlockSpec((1, L, width), lambda b, hb, s: (b, idx(s), col // width + hb))
        vec = pl.BlockSpec((1, width), lambda b, hb, s: (0, hb))
        in_specs = [tok(ODD_Q), tok(ODD_FB if rev else ODD_FF), tok(ODD_I), vec]
        args = [proj, proj, proj, lower_bound.reshape(1, -1)]
        if rev:
            in_specs += [tok(0), tok(ODD_G), vec]
            args += [prev, proj, hg_gain.reshape(1, -1)]
        return pl.pallas_call(
            functools.partial(_hgrn_kernel, rev=rev, heads_per_step=heads_per_step),
            grid=(nb, n_hb, n_lat + n_ctx),
            in_specs=in_specs,
            out_specs=tok(0),
            out_shape=jax.ShapeDtypeStruct((nb, t, HG_W), BF16 if rev else F32),
            scratch_shapes=[pltpu.VMEM((heads_per_step, HG_V_DIM, HG_K_DIM), F32),
                            pltpu.VMEM((L * HG_SUB, HG_K_DIM), F32),
                            pltpu.VMEM((L, HG_V_DIM), F32)],
            compiler_params=_params(("parallel", "parallel", "arbitrary")),
            name="hgrn_bwd" if rev else "hgrn_fwd",
        )(*args)

    return call(True, call(False, None))


def _conv_kernel(x_ref, w_ref, b_ref, o_ref, *, t_lat):
    x = x_ref[0]
    n = x.shape[0]
    row = lax.broadcasted_iota(jnp.int32, (n, 1), 0)
    acc = jnp.zeros_like(x) + b_ref[...]
    for j in range(SSD_CONV):
        d = j - SSD_CONV // 2
        src = row + d
        ok = (src >= 0) & (src < n) & ((row < t_lat) == (src < t_lat))
        shifted = x if d == 0 else pltpu.roll(x, (-d) % n, 0)
        acc += jnp.where(ok, shifted, 0.0) * w_ref[j:j + 1, :]
    o_ref[0] = _silu(acc)


def ssd_conv(proj, conv_w, conv_b, t_lat):
    nb, t, _ = proj.shape
    return pl.pallas_call(
        functools.partial(_conv_kernel, t_lat=t_lat),
        grid=(nb, SSD_XBC // LANES),
        in_specs=[pl.BlockSpec((1, t, LANES), lambda b, j: (b, 0, ODD_XBC // LANES + j)),
                  pl.BlockSpec((SSD_CONV, LANES), lambda b, j: (0, j)),
                  pl.BlockSpec((1, LANES), lambda b, j: (0, j))],
        out_specs=pl.BlockSpec((1, t, LANES), lambda b, j: (b, 0, j)),
        out_shape=jax.ShapeDtypeStruct((nb, t, SSD_XBC), F32),
        compiler_params=_params(("parallel", "parallel")),
        name="ssd_conv",
    )(proj, conv_w, conv_b.reshape(1, -1))


def _ssd_kernel(*refs, rev):
    if rev:
        (x_ref, bc_ref, gc_ref, gr_ref, dbr_ref, dbc_ref, alr_ref, alc_ref, prev_ref, z_ref, skip_ref, gain_ref,
         o_ref, s_ref, xw_ref, y_ref) = refs
    else:
        x_ref, bc_ref, gc_ref, gr_ref, dbr_ref, dbc_ref, alr_ref, alc_ref, o_ref, s_ref, xw_ref = refs
    L = SSD_CHUNK
    hpg = SSD_HEADS // SSD_GROUPS
    gw = hpg * SSD_HEAD_DIM

    @pl.when(pl.program_id(1) == 0)
    def _():
        s_ref[...] = jnp.zeros_like(s_ref)

    d0 = SSD_HEADS if rev else 0
    dt_col = _softplus(gc_ref[0][:, d0:d0 + SSD_HEADS] + dbr_ref[:, d0:d0 + SSD_HEADS])
    dt_row = _softplus(gr_ref[0, 0][d0:d0 + SSD_HEADS, :] + dbc_ref[d0:d0 + SSD_HEADS, :])
    a_col = dt_col * -jnp.exp(alr_ref[:, d0:d0 + SSD_HEADS])
    a_row = dt_row * -jnp.exp(alc_ref[d0:d0 + SSD_HEADS, :])
    mask = _time_tri(L, rev)
    tri = mask.astype(F32)
    ac_col = _dot_f32(tri, a_col)
    ac_row = _dot_nt_f32(a_row, tri)
    last = 0 if rev else L - 1
    x = x_ref[0]
    bc = bc_ref[0]

    for g in range(SSD_GROUPS):
        bm = bc[:, g * SSD_STATE:(g + 1) * SSD_STATE]
        cm = bc[:, SSD_BC_W + g * SSD_STATE:SSD_BC_W + (g + 1) * SSD_STATE]
        cb = _dot_nt(cm, bm)
        state = s_ref[g]
        inter = _dot_nt(cm, state)
        for hh in range(hpg):
            h = g * hpg + hh
            cols = slice(h * SSD_HEAD_DIM, (h + 1) * SSD_HEAD_DIM)
            gcols = slice(hh * SSD_HEAD_DIM, (hh + 1) * SSD_HEAD_DIM)
            acol, arow = ac_col[:, h:h + 1], ac_row[h:h + 1, :]
            decay = jnp.exp(jnp.where(mask, acol - arow, NEG))
            xh = x[:, cols]
            y = _dot(cb * decay * dt_row[h:h + 1, :], xh) + jnp.exp(acol) * inter[:, gcols]
            a_last = acol[last:last + 1, :]
            xw_ref[:, gcols] = xh * (jnp.exp(a_last - acol) * dt_col[:, h:h + 1])
            s_ref[g, gcols, :] = state[gcols, :] * jnp.exp(a_last)
            if rev:
                y_ref[:, cols] = y + prev_ref[0][:, cols]
            else:
                o_ref[0, :, cols] = y
        s_ref[g] += _dot_tn(xw_ref[...], bm)

    if rev:
        yt = (y_ref[...] + skip_ref[...] * x) * _silu(z_ref[0])
        for g in range(SSD_GROUPS):
            cols = slice(g * gw, (g + 1) * gw)
            o_ref[0, :, cols] = (_rms(yt[:, cols]) * gain_ref[:, cols]).astype(o_ref.dtype)


def ssd(proj, gates, xbc, dt_bias, a_log, d_skip, ssd_gain, t_lat):
    nb, t, _ = proj.shape
    L = SSD_CHUNK
    n_lat, n_ctx = t_lat // L, (t - t_lat) // L
    gate_rows = _gate_rows(gates, L)
    skip = jnp.repeat(d_skip.astype(F32), SSD_HEAD_DIM).reshape(1, SSD_W)

    def call(rev, prev):
        idx = lambda b, s: _chunk_index(s, n_lat, n_ctx, rev)
        tok = lambda width, col: pl.BlockSpec((1, L, width), lambda b, s: (b, idx(b, s), col // width))
        const = lambda shape: pl.BlockSpec(shape, lambda b, s: (0, 0))
        in_specs = [tok(SSD_W, 0), tok(2 * SSD_BC_W, SSD_W),
                    pl.BlockSpec((1, L, LANES), lambda b, s: (b, idx(b, s), 0)),
                    pl.BlockSpec((1, 1, GATE_ROWS, L), lambda b, s: (b, idx(b, s), 0, 0)),
                    const((1, LANES)), const((GATE_ROWS, 1)), const((1, LANES)), const((GATE_ROWS, 1))]
        args = [xbc, xbc, gates, gate_rows, _lane_row(dt_bias), _sublane_col(dt_bias),
                _lane_row(a_log), _sublane_col(a_log)]
        scratch = [pltpu.VMEM((SSD_GROUPS, SSD_W // SSD_GROUPS, SSD_STATE), F32),
                   pltpu.VMEM((L, SSD_W // SSD_GROUPS), F32)]
        if rev:
            in_specs += [tok(SSD_W, 0), tok(SSD_W, ODD_Z), const((1, SSD_W)), const((1, SSD_W))]
            args += [prev, proj, skip, ssd_gain.reshape(1, -1)]
            scratch += [pltpu.VMEM((L, SSD_W), F32)]
        return pl.pallas_call(
            functools.partial(_ssd_kernel, rev=rev),
            grid=(nb, n_lat + n_ctx),
            in_specs=in_specs,
            out_specs=tok(SSD_W, 0),
            out_shape=jax.ShapeDtypeStruct((nb, t, SSD_W), BF16 if rev else F32),
            scratch_shapes=scratch,
            compiler_params=_params(("parallel", "arbitrary")),
            name="ssd_bwd" if rev else "ssd_fwd",
        )(*args)

    return call(True, call(False, None))


def _split_w_in(w_in, n_gate):
    n_main = w_in.shape[1] - n_gate
    w_gate = jnp.zeros((w_in.shape[0], LANES), F32).at[:, :n_gate].set(w_in[:, n_main:])
    return w_in[:, :n_main].astype(BF16), w_gate


def kernel(x, c, ctx, c_ctx, hgrn_lb_logits,
           l0_mod_w, l0_mod_b, l0_w_in, l0_q_gain, l0_k_gain, l0_rpb, l0_gate_bias, l0_ml_gain,
           l0_w_out, l0_ffn_w1, l0_ffn_w3, l0_ffn_w2,
           l1_mod_w, l1_mod_b, l1_w_in, l1_hg_gain, l1_conv_w, l1_conv_b, l1_dt_bias, l1_a_log,
           l1_d_skip, l1_ssd_gain, l1_w_out, l1_ffn_w1, l1_ffn_w3, l1_ffn_w2):
    nb, t_lat, d = x.shape
    p = jax.nn.softmax(hgrn_lb_logits.astype(F32), axis=0)
    lower_bounds = jnp.cumsum(p, axis=0) - p[0]

    cvecs = jnp.zeros((8, d), F32).at[:nb].set(c).at[nb].set(c_ctx)
    xs = jnp.concatenate([x, ctx], axis=1)

    layers = [
        (l0_mod_w, l0_mod_b, l0_w_in, 4 * ML_HEADS, l0_w_out, l0_ffn_w1, l0_ffn_w3, l0_ffn_w2),
        (l1_mod_w, l1_mod_b, l1_w_in, 2 * SSD_HEADS, l1_w_out, l1_ffn_w1, l1_ffn_w3, l1_ffn_w2),
    ]
    for layer, (mod_w, mod_b, w_in, n_gate, w_out, w1, w3, w2) in enumerate(layers):
        mods = modulation_vectors(cvecs, mod_w, mod_b).reshape(8, N_MOD, 1, d)
        sh1, sc1, g1, sh2, sc2, g2 = [mods[:nb + 1, i] for i in range(N_MOD)]
        w_main, w_gate = _split_w_in(w_in, n_gate)
        proj, gates = modulated_in_proj(xs, sh1, sc1, w_main, w_gate, t_lat)
        if layer == 0:
            bias = na_bias_tables(l0_rpb, t_lat // GRID_W)
            ya = neighbourhood_attention(proj, bias, l0_q_gain, l0_k_gain, t_lat)
            yb = mlstm(proj, gates, l0_gate_bias, l0_ml_gain, t_lat)
        else:
            ya = hgrn2(proj, lower_bounds[layer], l1_hg_gain, t_lat)
            xbc = ssd_conv(proj, l1_conv_w, l1_conv_b, t_lat)
            yb = ssd(proj, gates, xbc, l1_dt_bias, l1_a_log, l1_d_skip, l1_ssd_gain, t_lat)
        ka = ya.shape[-1]
        t_out = xs.shape[1] if layer == 0 else t_lat
        xs = out_proj_residual(ya, yb, xs, g1, w_out[:ka].astype(BF16), w_out[ka:].astype(BF16), t_lat, t_out)
        xs = modulated_ffn_residual(xs, sh2, sc2, g2, w1.astype(BF16), w3.astype(BF16), w2.astype(BF16), t_lat)
    return xs
```

```python
import functools

import jax
import jax.numpy as jnp
import numpy as np
from jax import lax
from jax.experimental import pallas as pl
from jax.experimental.pallas import tpu as pltpu

EPS = 1e-6
N_MOD = 6
GRID_W = 64

NA_HEADS = 8
NA_HEAD_DIM = 128
NA_WIN_H = 8
NA_WIN_W = 16
NA_GROUP_ROWS = 4
NA_BAND_ROWS = NA_GROUP_ROWS + NA_WIN_H - 1
NA_GROUP = NA_GROUP_ROWS * GRID_W
NA_BAND = NA_BAND_ROWS * GRID_W

ML_HEADS = 4
ML_QK_DIM = 128
ML_V_DIM = 256
ML_CHUNK = 64

HG_HEADS = 8
HG_K_DIM = 128
HG_V_DIM = 128
HG_CHUNK = 32
HG_SUB = 8

SSD_HEADS = 16
SSD_HEAD_DIM = 64
SSD_GROUPS = 2
SSD_STATE = 128
SSD_CONV = 5
SSD_CHUNK = 64

NA_W = NA_HEADS * NA_HEAD_DIM
ML_QK_W = ML_HEADS * ML_QK_DIM
ML_W = ML_HEADS * ML_V_DIM
HG_K_W = HG_HEADS * HG_K_DIM
HG_W = HG_HEADS * HG_V_DIM
SSD_W = SSD_HEADS * SSD_HEAD_DIM
SSD_BC_W = SSD_GROUPS * SSD_STATE
SSD_XBC = SSD_W + 2 * SSD_BC_W

EVEN_Q, EVEN_K, EVEN_V = 0, NA_W, 2 * NA_W
EVEN_MQ = 3 * NA_W
EVEN_MK = EVEN_MQ + ML_QK_W
EVEN_MV = EVEN_MK + ML_QK_W
EVEN_MO = EVEN_MV + ML_W
ODD_Q, ODD_FF, ODD_FB = 0, HG_K_W, 2 * HG_K_W
ODD_I = 3 * HG_K_W
ODD_G = ODD_I + HG_W
ODD_Z = ODD_G + HG_W
ODD_XBC = ODD_Z + SSD_W

VMEM_LIMIT_BYTES = 56 * 1024 * 1024
LANES = 128
GATE_ROWS = 32
NEG = -1e30

F32 = jnp.float32
BF16 = jnp.bfloat16
HIGHEST = lax.Precision.HIGHEST


def _params(sem):
    return pltpu.CompilerParams(dimension_semantics=sem, vmem_limit_bytes=VMEM_LIMIT_BYTES)


def _dot(a, b):
    return jnp.dot(a.astype(BF16), b.astype(BF16), preferred_element_type=F32)


def _dot_nt(a, b):
    return lax.dot_general(a.astype(BF16), b.astype(BF16), (((1,), (1,)), ((), ())), preferred_element_type=F32)


def _dot_tn(a, b):
    return lax.dot_general(a.astype(BF16), b.astype(BF16), (((0,), (0,)), ((), ())), preferred_element_type=F32)


def _dot_f32(a, b):
    return jnp.dot(a, b, precision=HIGHEST, preferred_element_type=F32)


def _dot_nt_f32(a, b):
    return lax.dot_general(a, b, (((1,), (1,)), ((), ())), precision=HIGHEST, preferred_element_type=F32)


def _sigmoid(x):
    return 1.0 / (1.0 + jnp.exp(-x))


def _silu(x):
    return x * _sigmoid(x)


def _log_sigmoid(x):
    return jnp.minimum(x, 0.0) - jnp.log(1.0 + jnp.exp(-jnp.abs(x)))


def _softplus(x):
    return jnp.maximum(x, 0.0) + jnp.log(1.0 + jnp.exp(-jnp.abs(x)))


def _rms(x):
    return x * lax.rsqrt(jnp.mean(x * x, axis=-1, keepdims=True) + EPS)


def _time_tri(n, rev):
    t = lax.broadcasted_iota(jnp.int32, (n, n), 0)
    u = lax.broadcasted_iota(jnp.int32, (n, n), 1)
    return (u >= t) if rev else (u <= t)


def _chunk_index(step, n_lat, n_ctx, rev):
    if rev:
        return n_lat + n_ctx - 1 - step
    return jnp.where(step < n_ctx, n_lat + step, step - n_ctx)


def _mod_kernel(c_ref, w_ref, b_ref, o_ref):
    o_ref[...] = jnp.dot(_silu(c_ref[...]), w_ref[...], preferred_element_type=F32) + b_ref[...]


def modulation_vectors(cvecs, mod_w, mod_b, tn=1024):
    r, d = cvecs.shape
    n = mod_w.shape[1]
    return pl.pallas_call(
        _mod_kernel,
        grid=(n // tn,),
        in_specs=[pl.BlockSpec((r, d), lambda j: (0, 0)),
                  pl.BlockSpec((d, tn), lambda j: (0, j)),
                  pl.BlockSpec((1, tn), lambda j: (0, j))],
        out_specs=pl.BlockSpec((r, tn), lambda j: (0, j)),
        out_shape=jax.ShapeDtypeStruct((r, n), F32),
        compiler_params=_params(("arbitrary",)),
        name="modulation",
    )(cvecs, mod_w, mod_b.reshape(1, n))


def _mod_map(n_lat_blocks, n_batch):
    return lambda b, i, j: (jnp.where(i >= n_lat_blocks, n_batch, b), 0, 0)


def _inproj_kernel(x_ref, sh_ref, sc_ref, w_ref, wg_ref, o_ref, og_ref, h_ref):
    @pl.when(pl.program_id(2) == 0)
    def _():
        h = _rms(x_ref[0]) * (1.0 + sc_ref[0]) + sh_ref[0]
        h_ref[...] = h.astype(BF16)
        og_ref[0] = jnp.dot(h, wg_ref[...], preferred_element_type=F32)

    o_ref[0] = jnp.dot(h_ref[...], w_ref[...], preferred_element_type=F32)


def modulated_in_proj(x, shift, scale, w_main, w_gate, t_lat, tm=512, tn=512):
    nb, t, d = x.shape
    n = w_main.shape[1]
    mod_map = _mod_map(t_lat // tm, nb)
    return pl.pallas_call(
        _inproj_kernel,
        grid=(nb, pl.cdiv(t, tm), n // tn),
        in_specs=[pl.BlockSpec((1, tm, d), lambda b, i, j: (b, i, 0)),
                  pl.BlockSpec((1, 1, d), mod_map),
                  pl.BlockSpec((1, 1, d), mod_map),
                  pl.BlockSpec((d, tn), lambda b, i, j: (0, j)),
                  pl.BlockSpec((d, LANES), lambda b, i, j: (0, 0))],
        out_specs=[pl.BlockSpec((1, tm, tn), lambda b, i, j: (b, i, j)),
                   pl.BlockSpec((1, tm, LANES), lambda b, i, j: (b, i, 0))],
        out_shape=[jax.ShapeDtypeStruct((nb, t, n), F32),
                   jax.ShapeDtypeStruct((nb, t, LANES), F32)],
        scratch_shapes=[pltpu.VMEM((tm, d), BF16)],
        compiler_params=_params(("parallel", "parallel", "arbitrary")),
        name="in_proj",
    )(x, shift, scale, w_main, w_gate)


def _outproj_kernel(ya_ref, yb_ref, x_ref, g_ref, wa_ref, wb_ref, o_ref):
    acc = jnp.dot(ya_ref[0], wa_ref[...], preferred_element_type=F32)
    acc += jnp.dot(yb_ref[0], wb_ref[...], preferred_element_type=F32)
    o_ref[0] = x_ref[0] + g_ref[0] * acc


def out_proj_residual(ya, yb, x, gate, w_a, w_b, t_lat, t_out, tm=512, tn=1024):
    nb, _, k = ya.shape
    n = w_a.shape[1]
    mod_map = _mod_map(t_lat // tm, nb)
    return pl.pallas_call(
        _outproj_kernel,
        grid=(nb, pl.cdiv(t_out, tm), n // tn),
        in_specs=[pl.BlockSpec((1, tm, k), lambda b, i, j: (b, i, 0)),
                  pl.BlockSpec((1, tm, k), lambda b, i, j: (b, i, 0)),
                  pl.BlockSpec((1, tm, tn), lambda b, i, j: (b, i, j)),
                  pl.BlockSpec((1, 1, tn), lambda b, i, j: mod_map(b, i, j)[:2] + (j,)),
                  pl.BlockSpec((k, tn), lambda b, i, j: (0, j)),
                  pl.BlockSpec((k, tn), lambda b, i, j: (0, j))],
        out_specs=pl.BlockSpec((1, tm, tn), lambda b, i, j: (b, i, j)),
        out_shape=jax.ShapeDtypeStruct((nb, t_out, n), F32),
        compiler_params=_params(("parallel", "parallel", "arbitrary")),
        name="out_proj",
    )(ya, yb, x, gate, w_a, w_b)


def _ffn_kernel(x_ref, sh_ref, sc_ref, g_ref, w1_ref, w3_ref, w2_ref, o_ref, h_ref, acc_ref):
    j = pl.program_id(2)

    @pl.when(j == 0)
    def _():
        h_ref[...] = (_rms(x_ref[0]) * (1.0 + sc_ref[0]) + sh_ref[0]).astype(BF16)
        acc_ref[...] = jnp.zeros_like(acc_ref)

    h = h_ref[...]
    a = jnp.dot(h, w1_ref[...], preferred_element_type=F32)
    b = jnp.dot(h, w3_ref[...], preferred_element_type=F32)
    acc_ref[...] += jnp.dot((_silu(a) * b).astype(BF16), w2_ref[...], preferred_element_type=F32)

    @pl.when(j == pl.num_programs(2) - 1)
    def _():
        o_ref[0] = x_ref[0] + g_ref[0] * acc_ref[...]


def modulated_ffn_residual(x, shift, scale, gate, w1, w3, w2, t_lat, tm=512, th=512):
    nb, t, d = x.shape
    hid = w1.shape[1]
    mod_map = _mod_map(t_lat // tm, nb)
    return pl.pallas_call(
        _ffn_kernel,
        grid=(nb, pl.cdiv(t, tm), hid // th),
        in_specs=[pl.BlockSpec((1, tm, d), lambda b, i, j: (b, i, 0)),
                  pl.BlockSpec((1, 1, d), mod_map),
                  pl.BlockSpec((1, 1, d), mod_map),
                  pl.BlockSpec((1, 1, d), mod_map),
                  pl.BlockSpec((d, th), lambda b, i, j: (0, j)),
                  pl.BlockSpec((d, th), lambda b, i, j: (0, j)),
                  pl.BlockSpec((th, d), lambda b, i, j: (j, 0))],
        out_specs=pl.BlockSpec((1, tm, d), lambda b, i, j: (b, i, 0)),
        out_shape=jax.ShapeDtypeStruct((nb, t, d), F32),
        scratch_shapes=[pltpu.VMEM((tm, d), BF16), pltpu.VMEM((tm, d), F32)],
        compiler_params=_params(("parallel", "parallel", "arbitrary")),
        name="ffn",
    )(x, shift, scale, gate, w1, w3, w2)


def _na_geometry(rows):
    n_groups = rows // NA_GROUP_ROWS
    band0 = np.clip(np.arange(n_groups) * NA_GROUP_ROWS - NA_WIN_H // 2, 0, rows - NA_BAND_ROWS)
    return n_groups, band0


def na_bias_tables(rpb, rows):
    n_groups, band0 = _na_geometry(rows)
    tables = []
    for g in (0, 1, n_groups - 1):
        qr = g * NA_GROUP_ROWS + np.arange(NA_GROUP_ROWS)
        r0 = np.clip(qr - NA_WIN_H // 2, 0, rows - NA_WIN_H)
        kr = band0[g] + np.arange(NA_BAND_ROWS)
        qc = np.arange(GRID_W)
        kc = np.arange(GRID_W)
        w0 = np.clip(qc - NA_WIN_W // 2, 0, GRID_W - NA_WIN_W)
        row_ok = (kr[None, :] >= r0[:, None]) & (kr[None, :] < r0[:, None] + NA_WIN_H)
        col_ok = (kc[None, :] >= w0[:, None]) & (kc[None, :] < w0[:, None] + NA_WIN_W)
        drow = np.clip(kr[None, :] - qr[:, None] + NA_WIN_H - 1, 0, 2 * NA_WIN_H - 2)
        dcol = np.clip(kc[None, :] - qc[:, None], -(NA_WIN_W - 1), NA_WIN_W - 1) + NA_WIN_W - 1
        ok = row_ok[:, None, :, None] & col_ok[None, :, None, :]
        row_sel = np.eye(2 * NA_WIN_H - 1, dtype=np.float32)[drow]
        col_sel = np.eye(2 * NA_WIN_W - 1, dtype=np.float32)[dcol]
        by_row = jnp.einsum('hab,rka->hrkb', rpb.astype(F32), row_sel, precision=HIGHEST)
        bias = jnp.einsum('hrkb,cdb->hrckd', by_row, col_sel, precision=HIGHEST)
        tables.append(jnp.where(jnp.asarray(ok), bias, NEG).reshape(NA_HEADS, NA_GROUP, NA_BAND))
    return jnp.stack(tables)


def _na_kernel(q_ref, k_ref, v_ref, bias_ref, qg_ref, kg_ref, o_ref, *, rows, t_lat):
    g = pl.program_id(2)
    n_groups = rows // NA_GROUP_ROWS
    qn = (_rms(q_ref[0]) * qg_ref[...] * NA_HEAD_DIM ** -0.5).astype(BF16)
    kc = (_rms(k_ref[0, t_lat:, :]) * kg_ref[...]).astype(BF16)
    vc = v_ref[0, t_lat:, :].astype(BF16)
    s_ctx = _dot_nt(qn, kc)

    @pl.when(g < n_groups)
    def _():
        band0 = jnp.clip(g * NA_GROUP_ROWS - NA_WIN_H // 2, 0, rows - NA_BAND_ROWS)
        start = pl.multiple_of(band0 * GRID_W, GRID_W)
        kb = (_rms(k_ref[0, pl.ds(start, NA_BAND), :]) * kg_ref[...]).astype(BF16)
        vb = v_ref[0, pl.ds(start, NA_BAND), :].astype(BF16)
        s_loc = _dot_nt(qn, kb) + bias_ref[0, 0]
        m = jnp.maximum(jnp.max(s_loc, axis=-1, keepdims=True), jnp.max(s_ctx, axis=-1, keepdims=True))
        p_loc = jnp.exp(s_loc - m)
        p_ctx = jnp.exp(s_ctx - m)
        den = jnp.sum(p_loc, axis=-1, keepdims=True) + jnp.sum(p_ctx, axis=-1, keepdims=True)
        o_ref[0] = ((_dot(p_loc, vb) + _dot(p_ctx, vc)) / den).astype(o_ref.dtype)

    @pl.when(g >= n_groups)
    def _():
        p = jnp.exp(s_ctx - jnp.max(s_ctx, axis=-1, keepdims=True))
        o_ref[0] = (_dot(p, vc) / jnp.sum(p, axis=-1, keepdims=True)).astype(o_ref.dtype)


def neighbourhood_attention(proj, bias, q_gain, k_gain, t_lat):
    nb, t, _ = proj.shape
    rows = t_lat // GRID_W
    n_groups, _ = _na_geometry(rows)
    assert t - t_lat == NA_GROUP
    table = lambda b, h, g: (jnp.where(g == 0, 0, jnp.where(g >= n_groups - 1, 2, 1)), h, 0, 0)
    return pl.pallas_call(
        functools.partial(_na_kernel, rows=rows, t_lat=t_lat),
        grid=(nb, NA_HEADS, n_groups + 1),
        in_specs=[pl.BlockSpec((1, NA_GROUP, NA_HEAD_DIM), lambda b, h, g: (b, g, EVEN_Q // NA_HEAD_DIM + h)),
                  pl.BlockSpec((1, t, NA_HEAD_DIM), lambda b, h, g: (b, 0, EVEN_K // NA_HEAD_DIM + h)),
                  pl.BlockSpec((1, t, NA_HEAD_DIM), lambda b, h, g: (b, 0, EVEN_V // NA_HEAD_DIM + h)),
                  pl.BlockSpec((1, 1, NA_GROUP, NA_BAND), table),
                  pl.BlockSpec((1, NA_HEAD_DIM), lambda b, h, g: (0, 0)),
                  pl.BlockSpec((1, NA_HEAD_DIM), lambda b, h, g: (0, 0))],
        out_specs=pl.BlockSpec((1, NA_GROUP, NA_HEAD_DIM), lambda b, h, g: (b, g, h)),
        out_shape=jax.ShapeDtypeStruct((nb, t, NA_W), BF16),
        compiler_params=_params(("parallel", "parallel", "arbitrary")),
        name="natten",
    )(proj, proj, proj, bias, q_gain.reshape(1, -1), k_gain.reshape(1, -1))


def _mlstm_kernel(*refs, rev):
    if rev:
        (q_ref, k_ref, v_ref, gc_ref, gr_ref, gbr_ref, gbc_ref, prev_ref, og_ref, gain_ref,
         o_ref, c_ref, n_ref, m_ref) = refs
    else:
        q_ref, k_ref, v_ref, gc_ref, gr_ref, gbr_ref, gbc_ref, o_ref, c_ref, n_ref, m_ref = refs
    L = ML_CHUNK

    @pl.when(pl.program_id(1) == 0)
    def _():
        c_ref[...] = jnp.zeros_like(c_ref)
        n_ref[...] = jnp.zeros_like(n_ref)
        m_ref[...] = jnp.zeros_like(m_ref)

    base = 2 * ML_HEADS if rev else 0
    gcol = gc_ref[0][:, :4 * ML_HEADS] + gbr_ref[:, :4 * ML_HEADS]
    grow = gr_ref[0, 0][:4 * ML_HEADS, :] + gbc_ref[:4 * ML_HEADS, :]
    li_col = gcol[:, base:base + ML_HEADS]
    lf_col = _log_sigmoid(gcol[:, base + ML_HEADS:base + 2 * ML_HEADS])
    li_row = grow[base:base + ML_HEADS, :]
    lf_row = _log_sigmoid(grow[base + ML_HEADS:base + 2 * ML_HEADS, :])
    mask = _time_tri(L, rev)
    tri = mask.astype(F32)
    bc_col = _dot_f32(tri, lf_col)
    bc_row = _dot_nt_f32(lf_row, tri)
    last = 0 if rev else L - 1

    for h in range(ML_HEADS):
        bcol, brow = bc_col[:, h:h + 1], bc_row[h:h + 1, :]
        licol, lirow = li_col[:, h:h + 1], li_row[h:h + 1, :]
        m_prev = m_ref[h]
        q = q_ref[0][:, h * ML_QK_DIM:(h + 1) * ML_QK_DIM] * ML_QK_DIM ** -0.5
        k = k_ref[0][:, h * ML_QK_DIM:(h + 1) * ML_QK_DIM]
        v = v_ref[0][:, h * ML_V_DIM:(h + 1) * ML_V_DIM]
        dlog = jnp.where(mask, bcol - brow + lirow, NEG)
        inter = bcol + m_prev
        m_t = jnp.maximum(inter, jnp.max(dlog, axis=-1, keepdims=True))
        w_inter = jnp.exp(inter - m_t)
        s = _dot_nt(q, k) * jnp.exp(dlog - m_t)
        c_state, n_state = c_ref[h], n_ref[h]
        num = w_inter * _dot_nt(q, c_state) + _dot(s, v)
        den = w_inter * jnp.sum(q * n_state, axis=-1, keepdims=True) + jnp.sum(s, axis=-1, keepdims=True)
        hid = num / jnp.maximum(jnp.abs(den), jnp.exp(-m_t))
        b_last = bcol[last:last + 1, :]
        gg = b_last - bcol + licol
        m_new = jnp.maximum(b_last + m_prev, jnp.max(gg, axis=0, keepdims=True))
        decay = jnp.exp(b_last + m_prev - m_new)
        w = jnp.exp(gg - m_new)
        c_ref[h] = decay * c_state + _dot_tn(v * w, k)
        n_ref[h] = decay * n_state + jnp.sum(k * w, axis=0, keepdims=True)
        m_ref[h] = m_new
        cols = slice(h * ML_V_DIM, (h + 1) * ML_V_DIM)
        if rev:
            tot = hid + prev_ref[0][:, cols]
            y = _rms(tot) * gain_ref[:, cols] * _sigmoid(og_ref[0][:, cols])
            o_ref[0, :, cols] = y.astype(o_ref.dtype)
        else:
            o_ref[0, :, cols] = hid


def _gate_rows(gates, chunk):
    nb, t, _ = gates.shape
    return gates[..., :GATE_ROWS].reshape(nb, t // chunk, chunk, GATE_ROWS).transpose(0, 1, 3, 2)


def _lane_row(v):
    return jnp.zeros((1, LANES), F32).at[0, :v.size].set(v.reshape(-1).astype(F32))


def _sublane_col(v):
    return jnp.zeros((GATE_ROWS, 1), F32).at[:v.size, 0].set(v.reshape(-1).astype(F32))


def mlstm(proj, gates, gate_bias, ml_gain, t_lat):
    nb, t, _ = proj.shape
    L = ML_CHUNK
    n_lat, n_ctx = t_lat // L, (t - t_lat) // L
    gate_rows = _gate_rows(gates, L)
    gb_row, gb_col = _lane_row(gate_bias), _sublane_col(gate_bias)

    def call(rev, prev):
        idx = lambda b, s: _chunk_index(s, n_lat, n_ctx, rev)
        tok = lambda width, col: pl.BlockSpec((1, L, width), lambda b, s: (b, idx(b, s), col // width))
        in_specs = [tok(ML_QK_W, EVEN_MQ), tok(ML_QK_W, EVEN_MK), tok(ML_W, EVEN_MV),
                    pl.BlockSpec((1, L, LANES), lambda b, s: (b, idx(b, s), 0)),
                    pl.BlockSpec((1, 1, GATE_ROWS, L), lambda b, s: (b, idx(b, s), 0, 0)),
                    pl.BlockSpec((1, LANES), lambda b, s: (0, 0)),
                    pl.BlockSpec((GATE_ROWS, 1), lambda b, s: (0, 0))]
        args = [proj, proj, proj, gates, gate_rows, gb_row, gb_col]
        if rev:
            in_specs += [tok(ML_W, 0), tok(ML_W, EVEN_MO), pl.BlockSpec((1, ML_W), lambda b, s: (0, 0))]
            args += [prev, proj, ml_gain.reshape(1, -1)]
        return pl.pallas_call(
            functools.partial(_mlstm_kernel, rev=rev),
            grid=(nb, n_lat + n_ctx),
            in_specs=in_specs,
            out_specs=tok(ML_W, 0),
            out_shape=jax.ShapeDtypeStruct((nb, t, ML_W), BF16 if rev else F32),
            scratch_shapes=[pltpu.VMEM((ML_HEADS, ML_V_DIM, ML_QK_DIM), F32),
                            pltpu.VMEM((ML_HEADS, 1, ML_QK_DIM), F32),
                            pltpu.VMEM((ML_HEADS, 1, 1), F32)],
            compiler_params=_params(("parallel", "arbitrary")),
            name="mlstm_bwd" if rev else "mlstm_fwd",
        )(*args)

    return call(True, call(False, None))


def _hgrn_kernel(*refs, rev, heads_per_step):
    if rev:
        q_ref, f_ref, i_ref, lb_ref, prev_ref, og_ref, gain_ref, o_ref, s_ref, p_ref, od_ref = refs
    else:
        q_ref, f_ref, i_ref, lb_ref, o_ref, s_ref, p_ref, od_ref = refs
    L, SUB = HG_CHUNK, HG_SUB
    half = L // 2

    @pl.when(pl.program_id(2) == 0)
    def _():
        s_ref[...] = jnp.zeros_like(s_ref)

    mask = _time_tri(L, rev)
    tri = mask.astype(F32)
    t_id = lax.broadcasted_iota(jnp.int32, (L, L), 0)
    u_id = lax.broadcasted_iota(jnp.int32, (L, L), 1)
    row_id = lax.broadcasted_iota(jnp.int32, (L, 1), 0)
    sub_id = lax.broadcasted_iota(jnp.int32, (SUB, 1), 0)
    same_half = (t_id // half) == (u_id // half)
    if rev:
        m1 = (t_id < half) & (u_id >= half)
        m2 = same_half & (t_id % half < SUB) & (u_id % half >= SUB)
        r1, r2a, r2b, last = half, SUB, half + SUB, 0
    else:
        m1 = (t_id >= half) & (u_id < half)
        m2 = same_half & (t_id % half >= SUB) & (u_id % half < SUB)
        r1, r2a, r2b, last = half - 1, SUB - 1, half + SUB - 1, L - 1
    ones = jnp.ones((HG_K_DIM, LANES), BF16)

    for h in range(heads_per_step):
        cols = slice(h * HG_K_DIM, (h + 1) * HG_K_DIM)
        lb = lb_ref[:, cols]
        q = _silu(q_ref[0][:, cols])
        v = _silu(i_ref[0][:, cols])
        f = lb + (1.0 - lb) * _sigmoid(f_ref[0][:, cols])
        kk = 1.0 - f
        G = _dot_f32(tri, jnp.log(f))
        a1 = G[r1:r1 + 1, :]
        a2 = jnp.where(row_id < half, G[r2a:r2a + 1, :], G[r2b:r2b + 1, :])
        A1 = _dot_nt(q * jnp.exp(jnp.minimum(G - a1, 0.0)), kk * jnp.exp(jnp.minimum(a1 - G, 0.0)))
        A2 = _dot_nt(q * jnp.exp(jnp.minimum(G - a2, 0.0)), kk * jnp.exp(jnp.minimum(a2 - G, 0.0)))
        a_off = jnp.where(m1, A1, jnp.where(m2, A2, 0.0))
        for blk in range(L // SUB):
            kb = kk[blk * SUB:(blk + 1) * SUB, :]
            gb = G[blk * SUB:(blk + 1) * SUB, :]
            for t in range(SUB):
                r = blk * SUB + t
                ok = (sub_id >= t) if rev else (sub_id <= t)
                dec = jnp.exp(jnp.where(ok, G[r:r + 1, :] - gb, NEG))
                p_ref[h, r * SUB:(r + 1) * SUB, :] = q[r:r + 1, :] * kb * dec
        rsum = _dot(p_ref[h], ones)
        for blk in range(L // SUB):
            vb = v[blk * SUB:(blk + 1) * SUB, :]
            for t in range(SUB):
                r = blk * SUB + t
                od_ref[h, r:r + 1, :] = jnp.sum(rsum[r * SUB:(r + 1) * SUB, :] * vb, axis=0, keepdims=True)
        st = s_ref[h]
        o = _dot(a_off, v) + od_ref[h] + _dot_nt(q * jnp.exp(G), st)
        gl = G[last:last + 1, :]
        s_ref[h] = st * jnp.exp(gl) + _dot_tn(v, kk * jnp.exp(gl - G))
        if rev:
            tot = o + prev_ref[0][:, cols]
            y = _rms(tot) * gain_ref[:, cols] * _silu(og_ref[0][:, cols])
            o_ref[0, :, cols] = y.astype(o_ref.dtype)
        else:
            o_ref[0, :, cols] = o


def hgrn2(proj, lower_bound, hg_gain, t_lat, heads_per_step=HG_HEADS):
    nb, t, _ = proj.shape
    L = HG_CHUNK
    n_lat, n_ctx = t_lat // L, (t - t_lat) // L
    width = heads_per_step * HG_K_DIM
    n_hb = HG_HEADS // heads_per_step

    def call(rev, prev):
        idx = lambda s: _chunk_index(s, n_lat, n_ctx, rev)
        tok = lambda col: pl.BlockSpec((1, L, width), lambda b, hb, s: (b, idx(s), col // width + hb))
        vec = pl.BlockSpec((1, width), lambda b, hb, s: (0, hb))
        in_specs = [tok(ODD_Q), tok(ODD_FB if rev else ODD_FF), tok(ODD_I), vec]
        args = [proj, proj, proj, lower_bound.reshape(1, -1)]
        if rev:
            in_specs += [tok(0), tok(ODD_G), vec]
            args += [prev, proj, hg_gain.reshape(1, -1)]
        return pl.pallas_call(
            functools.partial(_hgrn_kernel, rev=rev, heads_per_step=heads_per_step),
            grid=(nb, n_hb, n_lat + n_ctx),
            in_specs=in_specs,
            out_specs=tok(0),
            out_shape=jax.ShapeDtypeStruct((nb, t, HG_W), BF16 if rev else F32),
            scratch_shapes=[pltpu.VMEM((heads_per_step, HG_V_DIM, HG_K_DIM), F32),
                            pltpu.VMEM((heads_per_step, L * HG_SUB, HG_K_DIM), F32),
                            pltpu.VMEM((heads_per_step, L, HG_V_DIM), F32)],
            compiler_params=_params(("parallel", "parallel", "arbitrary")),
            name="hgrn_bwd" if rev else "hgrn_fwd",
        )(*args)

    return call(True, call(False, None))


def _conv_kernel(x_ref, w_ref, b_ref, o_ref, *, t_lat):
    x = x_ref[0]
    n = x.shape[0]
    row = lax.broadcasted_iota(jnp.int32, (n, 1), 0)
    acc = jnp.zeros_like(x) + b_ref[...]
    for j in range(SSD_CONV):
        d = j - SSD_CONV // 2
        src = row + d
        ok = (src >= 0) & (src < n) & ((row < t_lat) == (src < t_lat))
        shifted = x if d == 0 else pltpu.roll(x, (-d) % n, 0)
        acc += jnp.where(ok, shifted, 0.0) * w_ref[j:j + 1, :]
    o_ref[0] = _silu(acc)


def ssd_conv(proj, conv_w, conv_b, t_lat):
    nb, t, _ = proj.shape
    return pl.pallas_call(
        functools.partial(_conv_kernel, t_lat=t_lat),
        grid=(nb, SSD_XBC // LANES),
        in_specs=[pl.BlockSpec((1, t, LANES), lambda b, j: (b, 0, ODD_XBC // LANES + j)),
                  pl.BlockSpec((SSD_CONV, LANES), lambda b, j: (0, j)),
                  pl.BlockSpec((1, LANES), lambda b, j: (0, j))],
        out_specs=pl.BlockSpec((1, t, LANES), lambda b, j: (b, 0, j)),
        out_shape=jax.ShapeDtypeStruct((nb, t, SSD_XBC), F32),
        compiler_params=_params(("parallel", "parallel")),
        name="ssd_conv",
    )(proj, conv_w, conv_b.reshape(1, -1))


def _ssd_kernel(*refs, rev):
    if rev:
        (x_ref, bc_ref, gc_ref, gr_ref, dbr_ref, dbc_ref, alr_ref, alc_ref, prev_ref, z_ref, skip_ref, gain_ref,
         o_ref, s_ref, xw_ref, y_ref) = refs
    else:
        x_ref, bc_ref, gc_ref, gr_ref, dbr_ref, dbc_ref, alr_ref, alc_ref, o_ref, s_ref, xw_ref = refs
    L = SSD_CHUNK
    hpg = SSD_HEADS // SSD_GROUPS
    gw = hpg * SSD_HEAD_DIM

    @pl.when(pl.program_id(1) == 0)
    def _():
        s_ref[...] = jnp.zeros_like(s_ref)

    d0 = SSD_HEADS if rev else 0
    dt_col = _softplus(gc_ref[0][:, d0:d0 + SSD_HEADS] + dbr_ref[:, d0:d0 + SSD_HEADS])
    dt_row = _softplus(gr_ref[0, 0][d0:d0 + SSD_HEADS, :] + dbc_ref[d0:d0 + SSD_HEADS, :])
    a_col = dt_col * -jnp.exp(alr_ref[:, d0:d0 + SSD_HEADS])
    a_row = dt_row * -jnp.exp(alc_ref[d0:d0 + SSD_HEADS, :])
    mask = _time_tri(L, rev)
    tri = mask.astype(F32)
    ac_col = _dot_f32(tri, a_col)
    ac_row = _dot_nt_f32(a_row, tri)
    last = 0 if rev else L - 1
    x = x_ref[0]
    bc = bc_ref[0]

    for g in range(SSD_GROUPS):
        bm = bc[:, g * SSD_STATE:(g + 1) * SSD_STATE]
        cm = bc[:, SSD_BC_W + g * SSD_STATE:SSD_BC_W + (g + 1) * SSD_STATE]
        cb = _dot_nt(cm, bm)
        state = s_ref[g]
        inter = _dot_nt(cm, state)
        for hh in range(hpg):
            h = g * hpg + hh
            cols = slice(h * SSD_HEAD_DIM, (h + 1) * SSD_HEAD_DIM)
            gcols = slice(hh * SSD_HEAD_DIM, (hh + 1) * SSD_HEAD_DIM)
            acol, arow = ac_col[:, h:h + 1], ac_row[h:h + 1, :]
            decay = jnp.exp(jnp.where(mask, acol - arow, NEG))
            xh = x[:, cols]
            y = _dot(cb * decay * dt_row[h:h + 1, :], xh) + jnp.exp(acol) * inter[:, gcols]
            a_last = acol[last:last + 1, :]
            xw_ref[:, gcols] = xh * (jnp.exp(a_last - acol) * dt_col[:, h:h + 1])
            s_ref[g, gcols, :] = state[gcols, :] * jnp.exp(a_last)
            if rev:
                y_ref[:, cols] = y + prev_ref[0][:, cols]
            else:
                o_ref[0, :, cols] = y
        s_ref[g] += _dot_tn(xw_ref[...], bm)

    if rev:
        yt = (y_ref[...] + skip_ref[...] * x) * _silu(z_ref[0])
        for g in range(SSD_GROUPS):
            cols = slice(g * gw, (g + 1) * gw)
            o_ref[0, :, cols] = (_rms(yt[:, cols]) * gain_ref[:, cols]).astype(o_ref.dtype)


def ssd(proj, gates, xbc, dt_bias, a_log, d_skip, ssd_gain, t_lat):
    nb, t, _ = proj.shape
    L = SSD_CHUNK
    n_lat, n_ctx = t_lat // L, (t - t_lat) // L
    gate_rows = _gate_rows(gates, L)
    skip = jnp.repeat(d_skip.astype(F32), SSD_HEAD_DIM).reshape(1, SSD_W)

    def call(rev, prev):
        idx = lambda b, s: _chunk_index(s, n_lat, n_ctx, rev)
        tok = lambda width, col: pl.BlockSpec((1, L, width), lambda b, s: (b, idx(b, s), col // width))
        const = lambda shape: pl.BlockSpec(shape, lambda b, s: (0, 0))
        in_specs = [tok(SSD_W, 0), tok(2 * SSD_BC_W, SSD_W),
                    pl.BlockSpec((1, L, LANES), lambda b, s: (b, idx(b, s), 0)),
                    pl.BlockSpec((1, 1, GATE_ROWS, L), lambda b, s: (b, idx(b, s), 0, 0)),
                    const((1, LANES)), const((GATE_ROWS, 1)), const((1, LANES)), const((GATE_ROWS, 1))]
        args = [xbc, xbc, gates, gate_rows, _lane_row(dt_bias), _sublane_col(dt_bias),
                _lane_row(a_log), _sublane_col(a_log)]
        scratch = [pltpu.VMEM((SSD_GROUPS, SSD_W // SSD_GROUPS, SSD_STATE), F32),
                   pltpu.VMEM((L, SSD_W // SSD_GROUPS), F32)]
        if rev:
            in_specs += [tok(SSD_W, 0), tok(SSD_W, ODD_Z), const((1, SSD_W)), const((1, SSD_W))]
            args += [prev, proj, skip, ssd_gain.reshape(1, -1)]
            scratch += [pltpu.VMEM((L, SSD_W), F32)]
        return pl.pallas_call(
            functools.partial(_ssd_kernel, rev=rev),
            grid=(nb, n_lat + n_ctx),
            in_specs=in_specs,
            out_specs=tok(SSD_W, 0),
            out_shape=jax.ShapeDtypeStruct((nb, t, SSD_W), BF16 if rev else F32),
            scratch_shapes=scratch,
            compiler_params=_params(("parallel", "arbitrary")),
            name="ssd_bwd" if rev else "ssd_fwd",
        )(*args)

    return call(True, call(False, None))


def _split_w_in(w_in, n_gate):
    n_main = w_in.shape[1] - n_gate
    w_gate = jnp.zeros((w_in.shape[0], LANES), F32).at[:, :n_gate].set(w_in[:, n_main:])
    return w_in[:, :n_main].astype(BF16), w_gate


def kernel(x, c, ctx, c_ctx, hgrn_lb_logits,
           l0_mod_w, l0_mod_b, l0_w_in, l0_q_gain, l0_k_gain, l0_rpb, l0_gate_bias, l0_ml_gain,
           l0_w_out, l0_ffn_w1, l0_ffn_w3, l0_ffn_w2,
           l1_mod_w, l1_mod_b, l1_w_in, l1_hg_gain, l1_conv_w, l1_conv_b, l1_dt_bias, l1_a_log,
           l1_d_skip, l1_ssd_gain, l1_w_out, l1_ffn_w1, l1_ffn_w3, l1_ffn_w2):
    nb, t_lat, d = x.shape
    p = jax.nn.softmax(hgrn_lb_logits.astype(F32), axis=0)
    lower_bounds = jnp.cumsum(p, axis=0) - p[0]

    cvecs = jnp.zeros((8, d), F32).at[:nb].set(c).at[nb].set(c_ctx)
    xs = jnp.concatenate([x, ctx], axis=1)

    layers = [
        (l0_mod_w, l0_mod_b, l0_w_in, 4 * ML_HEADS, l0_w_out, l0_ffn_w1, l0_ffn_w3, l0_ffn_w2),
        (l1_mod_w, l1_mod_b, l1_w_in, 2 * SSD_HEADS, l1_w_out, l1_ffn_w1, l1_ffn_w3, l1_ffn_w2),
    ]
    for layer, (mod_w, mod_b, w_in, n_gate, w_out, w1, w3, w2) in enumerate(layers):
        mods = modulation_vectors(cvecs, mod_w, mod_b).reshape(8, N_MOD, 1, d)
        sh1, sc1, g1, sh2, sc2, g2 = [mods[:nb + 1, i] for i in range(N_MOD)]
        w_main, w_gate = _split_w_in(w_in, n_gate)
        proj, gates = modulated_in_proj(xs, sh1, sc1, w_main, w_gate, t_lat)
        if layer == 0:
            bias = na_bias_tables(l0_rpb, t_lat // GRID_W)
            ya = neighbourhood_attention(proj, bias, l0_q_gain, l0_k_gain, t_lat)
            yb = mlstm(proj, gates, l0_gate_bias, l0_ml_gain, t_lat)
        else:
            ya = hgrn2(proj, lower_bounds[layer], l1_hg_gain, t_lat)
            xbc = ssd_conv(proj, l1_conv_w, l1_conv_b, t_lat)
            yb = ssd(proj, gates, xbc, l1_dt_bias, l1_a_log, l1_d_skip, l1_ssd_gain, t_lat)
        ka = ya.shape[-1]
        t_out = xs.shape[1] if layer == 0 else t_lat
        xs = out_proj_residual(ya, yb, xs, g1, w_out[:ka].astype(BF16), w_out[ka:].astype(BF16), t_lat, t_out)
        xs = modulated_ffn_residual(xs, sh2, sc2, g2, w1.astype(BF16), w3.astype(BF16), w2.astype(BF16), t_lat)
    return xs
```

```python
import functools

import jax
import jax.numpy as jnp
import numpy as np
from jax import lax
from jax.experimental import pallas as pl
from jax.experimental.pallas import tpu as pltpu

EPS = 1e-6
N_MOD = 6
GRID_W = 64

NA_HEADS = 8
NA_HEAD_DIM = 128
NA_WIN_H = 8
NA_WIN_W = 16
NA_GROUP_ROWS = 4
NA_BAND_ROWS = NA_GROUP_ROWS + NA_WIN_H - 1
NA_GROUP = NA_GROUP_ROWS * GRID_W
NA_BAND = NA_BAND_ROWS * GRID_W

ML_HEADS = 4
ML_QK_DIM = 128
ML_V_DIM = 256
ML_CHUNK = 64

HG_HEADS = 8
HG_K_DIM = 128
HG_V_DIM = 128
HG_CHUNK = 32
HG_SUB = 8

SSD_HEADS = 16
SSD_HEAD_DIM = 64
SSD_GROUPS = 2
SSD_STATE = 128
SSD_CONV = 5
SSD_CHUNK = 64

NA_W = NA_HEADS * NA_HEAD_DIM
ML_QK_W = ML_HEADS * ML_QK_DIM
ML_W = ML_HEADS * ML_V_DIM
HG_K_W = HG_HEADS * HG_K_DIM
HG_W = HG_HEADS * HG_V_DIM
SSD_W = SSD_HEADS * SSD_HEAD_DIM
SSD_BC_W = SSD_GROUPS * SSD_STATE
SSD_XBC = SSD_W + 2 * SSD_BC_W

EVEN_Q, EVEN_K, EVEN_V = 0, NA_W, 2 * NA_W
EVEN_MQ = 3 * NA_W
EVEN_MK = EVEN_MQ + ML_QK_W
EVEN_MV = EVEN_MK + ML_QK_W
EVEN_MO = EVEN_MV + ML_W
ODD_Q, ODD_FF, ODD_FB = 0, HG_K_W, 2 * HG_K_W
ODD_I = 3 * HG_K_W
ODD_G = ODD_I + HG_W
ODD_Z = ODD_G + HG_W
ODD_XBC = ODD_Z + SSD_W

VMEM_LIMIT_BYTES = 56 * 1024 * 1024
LANES = 128
GATE_ROWS = 32
NEG = -1e30
ML_SAMPLES_PER_STEP = 2
HG_SAMPLES_PER_STEP = 2
SSD_SAMPLES_PER_STEP = 1

F32 = jnp.float32
BF16 = jnp.bfloat16
HIGHEST = lax.Precision.HIGHEST


def _params(sem):
    return pltpu.CompilerParams(dimension_semantics=sem, vmem_limit_bytes=VMEM_LIMIT_BYTES)


def _dot(a, b):
    return jnp.dot(a.astype(BF16), b.astype(BF16), preferred_element_type=F32)


def _dot_nt(a, b):
    return lax.dot_general(a.astype(BF16), b.astype(BF16), (((1,), (1,)), ((), ())), preferred_element_type=F32)


def _dot_tn(a, b):
    return lax.dot_general(a.astype(BF16), b.astype(BF16), (((0,), (0,)), ((), ())), preferred_element_type=F32)


def _dot_f32(a, b):
    return jnp.dot(a, b, precision=HIGHEST, preferred_element_type=F32)


def _dot_nt_f32(a, b):
    return lax.dot_general(a, b, (((1,), (1,)), ((), ())), precision=HIGHEST, preferred_element_type=F32)


def _sigmoid(x):
    return 1.0 / (1.0 + jnp.exp(-x))


def _silu(x):
    return x * _sigmoid(x)


def _log_sigmoid(x):
    return jnp.minimum(x, 0.0) - jnp.log(1.0 + jnp.exp(-jnp.abs(x)))


def _softplus(x):
    return jnp.maximum(x, 0.0) + jnp.log(1.0 + jnp.exp(-jnp.abs(x)))


def _rms(x):
    return x * lax.rsqrt(jnp.mean(x * x, axis=-1, keepdims=True) + EPS)


def _time_tri(n, rev):
    t = lax.broadcasted_iota(jnp.int32, (n, n), 0)
    u = lax.broadcasted_iota(jnp.int32, (n, n), 1)
    return (u >= t) if rev else (u <= t)


def _chunk_index(step, n_lat, n_ctx, rev):
    if rev:
        return n_lat + n_ctx - 1 - step
    return jnp.where(step < n_ctx, n_lat + step, step - n_ctx)


def _mod_kernel(c_ref, w_ref, b_ref, o_ref):
    o_ref[...] = jnp.dot(_silu(c_ref[...]), w_ref[...], preferred_element_type=F32) + b_ref[...]


def modulation_vectors(cvecs, mod_w, mod_b, tn=1024):
    r, d = cvecs.shape
    n = mod_w.shape[1]
    return pl.pallas_call(
        _mod_kernel,
        grid=(n // tn,),
        in_specs=[pl.BlockSpec((r, d), lambda j: (0, 0)),
                  pl.BlockSpec((d, tn), lambda j: (0, j)),
                  pl.BlockSpec((1, tn), lambda j: (0, j))],
        out_specs=pl.BlockSpec((r, tn), lambda j: (0, j)),
        out_shape=jax.ShapeDtypeStruct((r, n), F32),
        compiler_params=_params(("arbitrary",)),
        name="modulation",
    )(cvecs, mod_w, mod_b.reshape(1, n))


def _mod_map(n_lat_blocks, n_batch):
    return lambda b, i, j: (jnp.where(i >= n_lat_blocks, n_batch, b), 0, 0)


def _inproj_kernel(x_ref, sh_ref, sc_ref, w_ref, wg_ref, o_ref, og_ref, h_ref):
    @pl.when(pl.program_id(2) == 0)
    def _():
        h = _rms(x_ref[0]) * (1.0 + sc_ref[0]) + sh_ref[0]
        h_ref[...] = h.astype(BF16)
        og_ref[0] = jnp.dot(h, wg_ref[...], preferred_element_type=F32)

    o_ref[0] = jnp.dot(h_ref[...], w_ref[...], preferred_element_type=F32)


def modulated_in_proj(x, shift, scale, w_main, w_gate, t_lat, tm=512, tn=1536):
    nb, t, d = x.shape
    n = w_main.shape[1]
    mod_map = _mod_map(t_lat // tm, nb)
    return pl.pallas_call(
        _inproj_kernel,
        grid=(nb, pl.cdiv(t, tm), n // tn),
        in_specs=[pl.BlockSpec((1, tm, d), lambda b, i, j: (b, i, 0)),
                  pl.BlockSpec((1, 1, d), mod_map),
                  pl.BlockSpec((1, 1, d), mod_map),
                  pl.BlockSpec((d, tn), lambda b, i, j: (0, j)),
                  pl.BlockSpec((d, LANES), lambda b, i, j: (0, 0))],
        out_specs=[pl.BlockSpec((1, tm, tn), lambda b, i, j: (b, i, j)),
                   pl.BlockSpec((1, tm, LANES), lambda b, i, j: (b, i, 0))],
        out_shape=[jax.ShapeDtypeStruct((nb, t, n), F32),
                   jax.ShapeDtypeStruct((nb, t, LANES), F32)],
        scratch_shapes=[pltpu.VMEM((tm, d), BF16)],
        compiler_params=_params(("parallel", "parallel", "arbitrary")),
        name="in_proj",
    )(x, shift, scale, w_main, w_gate)


def _outproj_kernel(ya_ref, yb_ref, x_ref, g_ref, wa_ref, wb_ref, o_ref):
    acc = jnp.dot(ya_ref[0], wa_ref[...], preferred_element_type=F32)
    acc += jnp.dot(yb_ref[0], wb_ref[...], preferred_element_type=F32)
    o_ref[0] = x_ref[0] + g_ref[0] * acc


def out_proj_residual(ya, yb, x, gate, w_a, w_b, t_lat, t_out, tm=512, tn=1024):
    nb, _, k = ya.shape
    n = w_a.shape[1]
    mod_map = _mod_map(t_lat // tm, nb)
    return pl.pallas_call(
        _outproj_kernel,
        grid=(nb, pl.cdiv(t_out, tm), n // tn),
        in_specs=[pl.BlockSpec((1, tm, k), lambda b, i, j: (b, i, 0)),
                  pl.BlockSpec((1, tm, k), lambda b, i, j: (b, i, 0)),
                  pl.BlockSpec((1, tm, tn), lambda b, i, j: (b, i, j)),
                  pl.BlockSpec((1, 1, tn), lambda b, i, j: mod_map(b, i, j)[:2] + (j,)),
                  pl.BlockSpec((k, tn), lambda b, i, j: (0, j)),
                  pl.BlockSpec((k, tn), lambda b, i, j: (0, j))],
        out_specs=pl.BlockSpec((1, tm, tn), lambda b, i, j: (b, i, j)),
        out_shape=jax.ShapeDtypeStruct((nb, t_out, n), F32),
        compiler_params=_params(("parallel", "parallel", "arbitrary")),
        name="out_proj",
    )(ya, yb, x, gate, w_a, w_b)


def _ffn_kernel(x_ref, sh_ref, sc_ref, g_ref, w1_ref, w3_ref, w2_ref, o_ref, h_ref, acc_ref):
    j = pl.program_id(2)

    @pl.when(j == 0)
    def _():
        h_ref[...] = (_rms(x_ref[0]) * (1.0 + sc_ref[0]) + sh_ref[0]).astype(BF16)
        acc_ref[...] = jnp.zeros_like(acc_ref)

    h = h_ref[...]
    a = jnp.dot(h, w1_ref[...], preferred_element_type=F32)
    b = jnp.dot(h, w3_ref[...], preferred_element_type=F32)
    acc_ref[...] += jnp.dot((_silu(a) * b).astype(BF16), w2_ref[...], preferred_element_type=F32)

    @pl.when(j == pl.num_programs(2) - 1)
    def _():
        o_ref[0] = x_ref[0] + g_ref[0] * acc_ref[...]


def modulated_ffn_residual(x, shift, scale, gate, w1, w3, w2, t_lat, tm=512, th=512):
    nb, t, d = x.shape
    hid = w1.shape[1]
    mod_map = _mod_map(t_lat // tm, nb)
    return pl.pallas_call(
        _ffn_kernel,
        grid=(nb, pl.cdiv(t, tm), hid // th),
        in_specs=[pl.BlockSpec((1, tm, d), lambda b, i, j: (b, i, 0)),
                  pl.BlockSpec((1, 1, d), mod_map),
                  pl.BlockSpec((1, 1, d), mod_map),
                  pl.BlockSpec((1, 1, d), mod_map),
                  pl.BlockSpec((d, th), lambda b, i, j: (0, j)),
                  pl.BlockSpec((d, th), lambda b, i, j: (0, j)),
                  pl.BlockSpec((th, d), lambda b, i, j: (j, 0))],
        out_specs=pl.BlockSpec((1, tm, d), lambda b, i, j: (b, i, 0)),
        out_shape=jax.ShapeDtypeStruct((nb, t, d), F32),
        scratch_shapes=[pltpu.VMEM((tm, d), BF16), pltpu.VMEM((tm, d), F32)],
        compiler_params=_params(("parallel", "parallel", "arbitrary")),
        name="ffn",
    )(x, shift, scale, gate, w1, w3, w2)


def _na_geometry(rows):
    n_groups = rows // NA_GROUP_ROWS
    band0 = np.clip(np.arange(n_groups) * NA_GROUP_ROWS - NA_WIN_H // 2, 0, rows - NA_BAND_ROWS)
    return n_groups, band0


def na_bias_tables(rpb, rows):
    n_groups, band0 = _na_geometry(rows)
    tables = []
    for g in (0, 1, n_groups - 1):
        qr = g * NA_GROUP_ROWS + np.arange(NA_GROUP_ROWS)
        r0 = np.clip(qr - NA_WIN_H // 2, 0, rows - NA_WIN_H)
        kr = band0[g] + np.arange(NA_BAND_ROWS)
        qc = np.arange(GRID_W)
        kc = np.arange(GRID_W)
        w0 = np.clip(qc - NA_WIN_W // 2, 0, GRID_W - NA_WIN_W)
        row_ok = (kr[None, :] >= r0[:, None]) & (kr[None, :] < r0[:, None] + NA_WIN_H)
        col_ok = (kc[None, :] >= w0[:, None]) & (kc[None, :] < w0[:, None] + NA_WIN_W)
        drow = np.clip(kr[None, :] - qr[:, None] + NA_WIN_H - 1, 0, 2 * NA_WIN_H - 2)
        dcol = np.clip(kc[None, :] - qc[:, None], -(NA_WIN_W - 1), NA_WIN_W - 1) + NA_WIN_W - 1
        ok = row_ok[:, None, :, None] & col_ok[None, :, None, :]
        row_sel = np.eye(2 * NA_WIN_H - 1, dtype=np.float32)[drow]
        col_sel = np.eye(2 * NA_WIN_W - 1, dtype=np.float32)[dcol]
        by_row = jnp.einsum('hab,rka->hrkb', rpb.astype(F32), row_sel, precision=HIGHEST)
        bias = jnp.einsum('hrkb,cdb->hrckd', by_row, col_sel, precision=HIGHEST)
        tables.append(jnp.where(jnp.asarray(ok), bias, NEG).reshape(NA_HEADS, NA_GROUP, NA_BAND))
    return jnp.stack(tables)


def _na_kernel(q_ref, k_ref, v_ref, bias_ref, qg_ref, kg_ref, o_ref, *, rows, t_lat):
    g = pl.program_id(2)
    n_groups = rows // NA_GROUP_ROWS
    qn = (_rms(q_ref[0]) * qg_ref[...] * NA_HEAD_DIM ** -0.5).astype(BF16)
    kc = (_rms(k_ref[0, t_lat:, :]) * kg_ref[...]).astype(BF16)
    vc = v_ref[0, t_lat:, :].astype(BF16)
    s_ctx = _dot_nt(qn, kc)

    @pl.when(g < n_groups)
    def _():
        band0 = jnp.clip(g * NA_GROUP_ROWS - NA_WIN_H // 2, 0, rows - NA_BAND_ROWS)
        start = pl.multiple_of(band0 * GRID_W, GRID_W)
        kb = (_rms(k_ref[0, pl.ds(start, NA_BAND), :]) * kg_ref[...]).astype(BF16)
        vb = v_ref[0, pl.ds(start, NA_BAND), :].astype(BF16)
        s_loc = _dot_nt(qn, kb) + bias_ref[0, 0]
        m = jnp.maximum(jnp.max(s_loc, axis=-1, keepdims=True), jnp.max(s_ctx, axis=-1, keepdims=True))
        p_loc = jnp.exp(s_loc - m)
        p_ctx = jnp.exp(s_ctx - m)
        den = jnp.sum(p_loc, axis=-1, keepdims=True) + jnp.sum(p_ctx, axis=-1, keepdims=True)
        o_ref[0] = ((_dot(p_loc, vb) + _dot(p_ctx, vc)) / den).astype(o_ref.dtype)

    @pl.when(g >= n_groups)
    def _():
        p = jnp.exp(s_ctx - jnp.max(s_ctx, axis=-1, keepdims=True))
        o_ref[0] = (_dot(p, vc) / jnp.sum(p, axis=-1, keepdims=True)).astype(o_ref.dtype)


def neighbourhood_attention(proj, bias, q_gain, k_gain, t_lat):
    nb, t, _ = proj.shape
    rows = t_lat // GRID_W
    n_groups, _ = _na_geometry(rows)
    assert t - t_lat == NA_GROUP
    table = lambda b, h, g: (jnp.where(g == 0, 0, jnp.where(g >= n_groups - 1, 2, 1)), h, 0, 0)
    return pl.pallas_call(
        functools.partial(_na_kernel, rows=rows, t_lat=t_lat),
        grid=(nb, NA_HEADS, n_groups + 1),
        in_specs=[pl.BlockSpec((1, NA_GROUP, NA_HEAD_DIM), lambda b, h, g: (b, g, EVEN_Q // NA_HEAD_DIM + h)),
                  pl.BlockSpec((1, t, NA_HEAD_DIM), lambda b, h, g: (b, 0, EVEN_K // NA_HEAD_DIM + h)),
                  pl.BlockSpec((1, t, NA_HEAD_DIM), lambda b, h, g: (b, 0, EVEN_V // NA_HEAD_DIM + h)),
                  pl.BlockSpec((1, 1, NA_GROUP, NA_BAND), table),
                  pl.BlockSpec((1, NA_HEAD_DIM), lambda b, h, g: (0, 0)),
                  pl.BlockSpec((1, NA_HEAD_DIM), lambda b, h, g: (0, 0))],
        out_specs=pl.BlockSpec((1, NA_GROUP, NA_HEAD_DIM), lambda b, h, g: (b, g, h)),
        out_shape=jax.ShapeDtypeStruct((nb, t, NA_W), BF16),
        compiler_params=_params(("parallel", "parallel", "arbitrary")),
        name="natten",
    )(proj, proj, proj, bias, q_gain.reshape(1, -1), k_gain.reshape(1, -1))


def _mlstm_kernel(*refs, rev):
    if rev:
        (q_ref, k_ref, v_ref, gc_ref, gr_ref, gbr_ref, gbc_ref, prev_ref, og_ref, gain_ref,
         o_ref, c_ref, n_ref, m_ref) = refs
    else:
        q_ref, k_ref, v_ref, gc_ref, gr_ref, gbr_ref, gbc_ref, o_ref, c_ref, n_ref, m_ref = refs
    L = ML_CHUNK

    base = 2 * ML_HEADS if rev else 0
    gcol = gc_ref[0][:, :4 * ML_HEADS] + gbr_ref[:, :4 * ML_HEADS]
    grow = gr_ref[0, 0][:4 * ML_HEADS, :] + gbc_ref[:4 * ML_HEADS, :]
    li_col = gcol[:, base:base + ML_HEADS]
    lf_col = _log_sigmoid(gcol[:, base + ML_HEADS:base + 2 * ML_HEADS])
    li_row = grow[base:base + ML_HEADS, :]
    lf_row = _log_sigmoid(grow[base + ML_HEADS:base + 2 * ML_HEADS, :])
    mask = _time_tri(L, rev)
    tri = mask.astype(F32)
    bc_col = _dot_f32(tri, lf_col)
    bc_row = _dot_nt_f32(lf_row, tri)
    last = 0 if rev else L - 1

    for h in range(ML_HEADS):
        bcol, brow = bc_col[:, h:h + 1], bc_row[h:h + 1, :]
        licol, lirow = li_col[:, h:h + 1], li_row[h:h + 1, :]
        m_prev = m_ref[h]
        q = q_ref[0][:, h * ML_QK_DIM:(h + 1) * ML_QK_DIM] * ML_QK_DIM ** -0.5
        k = k_ref[0][:, h * ML_QK_DIM:(h + 1) * ML_QK_DIM]
        v = v_ref[0][:, h * ML_V_DIM:(h + 1) * ML_V_DIM]
        dlog = jnp.where(mask, bcol - brow + lirow, NEG)
        inter = bcol + m_prev
        m_t = jnp.maximum(inter, jnp.max(dlog, axis=-1, keepdims=True))
        w_inter = jnp.exp(inter - m_t)
        s = _dot_nt(q, k) * jnp.exp(dlog - m_t)
        c_state, n_state = c_ref[h], n_ref[h]
        num = w_inter * _dot_nt(q, c_state) + _dot(s, v)
        den = w_inter * jnp.sum(q * n_state, axis=-1, keepdims=True) + jnp.sum(s, axis=-1, keepdims=True)
        hid = num / jnp.maximum(jnp.abs(den), jnp.exp(-m_t))
        b_last = bcol[last:last + 1, :]
        gg = b_last - bcol + licol
        m_new = jnp.maximum(b_last + m_prev, jnp.max(gg, axis=0, keepdims=True))
        decay = jnp.exp(b_last + m_prev - m_new)
        w = jnp.exp(gg - m_new)
        c_ref[h] = decay * c_state + _dot_tn(v * w, k)
        n_ref[h] = decay * n_state + jnp.sum(k * w, axis=0, keepdims=True)
        m_ref[h] = m_new
        cols = slice(h * ML_V_DIM, (h + 1) * ML_V_DIM)
        if rev:
            tot = hid + prev_ref[0][:, cols]
            y = _rms(tot) * gain_ref[:, cols] * _sigmoid(og_ref[0][:, cols])
            o_ref[0, :, cols] = y.astype(o_ref.dtype)
        else:
            o_ref[0, :, cols] = hid


def _for_each_sample(body, per_sample, n_scratch, step_axis):
    def kern(*refs):
        n_io = len(refs) - n_scratch

        @pl.when(pl.program_id(step_axis) == 0)
        def _():
            for r in refs[n_io:]:
                r[...] = jnp.zeros_like(r)

        for bb in range(refs[-1].shape[0]):
            views = [r.at[bb:bb + 1] if per_sample[i] else r for i, r in enumerate(refs[:n_io])]
            body(*views, *[r.at[bb] for r in refs[n_io:]])
    return kern


def _sample_scratch(n_samples, shape):
    return pltpu.VMEM((n_samples,) + shape, F32)


def _gate_rows(gates, chunk):
    nb, t, _ = gates.shape
    return gates[..., :GATE_ROWS].reshape(nb, t // chunk, chunk, GATE_ROWS).transpose(0, 1, 3, 2)


def _lane_row(v):
    return jnp.zeros((1, LANES), F32).at[0, :v.size].set(v.reshape(-1).astype(F32))


def _sublane_col(v):
    return jnp.zeros((GATE_ROWS, 1), F32).at[:v.size, 0].set(v.reshape(-1).astype(F32))


def mlstm(proj, gates, gate_bias, ml_gain, t_lat):
    nb, t, _ = proj.shape
    L = ML_CHUNK
    n_lat, n_ctx = t_lat // L, (t - t_lat) // L
    gate_rows = _gate_rows(gates, L)
    gb_row, gb_col = _lane_row(gate_bias), _sublane_col(gate_bias)

    def call(rev, prev):
        idx = lambda b, s: _chunk_index(s, n_lat, n_ctx, rev)
        ns = ML_SAMPLES_PER_STEP
        tok = lambda width, col: pl.BlockSpec((ns, L, width), lambda b, s: (b, idx(b, s), col // width))
        in_specs = [tok(ML_QK_W, EVEN_MQ), tok(ML_QK_W, EVEN_MK), tok(ML_W, EVEN_MV),
                    pl.BlockSpec((ns, L, LANES), lambda b, s: (b, idx(b, s), 0)),
                    pl.BlockSpec((ns, 1, GATE_ROWS, L), lambda b, s: (b, idx(b, s), 0, 0)),
                    pl.BlockSpec((1, LANES), lambda b, s: (0, 0)),
                    pl.BlockSpec((GATE_ROWS, 1), lambda b, s: (0, 0))]
        args = [proj, proj, proj, gates, gate_rows, gb_row, gb_col]
        per_sample = [True] * 5 + [False] * 2
        if rev:
            in_specs += [tok(ML_W, 0), tok(ML_W, EVEN_MO), pl.BlockSpec((1, ML_W), lambda b, s: (0, 0))]
            args += [prev, proj, ml_gain.reshape(1, -1)]
            per_sample += [True, True, False]
        return pl.pallas_call(
            _for_each_sample(functools.partial(_mlstm_kernel, rev=rev), per_sample + [True], 3, step_axis=1),
            grid=(nb // ns, n_lat + n_ctx),
            in_specs=in_specs,
            out_specs=tok(ML_W, 0),
            out_shape=jax.ShapeDtypeStruct((nb, t, ML_W), BF16 if rev else F32),
            scratch_shapes=[_sample_scratch(ns, (ML_HEADS, ML_V_DIM, ML_QK_DIM)),
                            _sample_scratch(ns, (ML_HEADS, 1, ML_QK_DIM)),
                            _sample_scratch(ns, (ML_HEADS, 1, 1))],
            compiler_params=_params(("parallel", "arbitrary")),
            name="mlstm_bwd" if rev else "mlstm_fwd",
        )(*args)

    return call(True, call(False, None))


def _hgrn_kernel(*refs, rev, heads_per_step):
    if rev:
        q_ref, f_ref, i_ref, lb_ref, prev_ref, og_ref, gain_ref, o_ref, s_ref, p_ref, od_ref = refs
    else:
        q_ref, f_ref, i_ref, lb_ref, o_ref, s_ref, p_ref, od_ref = refs
    L, SUB = HG_CHUNK, HG_SUB
    half = L // 2

    mask = _time_tri(L, rev)
    tri = mask.astype(F32)
    t_id = lax.broadcasted_iota(jnp.int32, (L, L), 0)
    u_id = lax.broadcasted_iota(jnp.int32, (L, L), 1)
    row_id = lax.broadcasted_iota(jnp.int32, (L, 1), 0)
    sub_id = lax.broadcasted_iota(jnp.int32, (SUB, 1), 0)
    same_half = (t_id // half) == (u_id // half)
    if rev:
        m1 = (t_id < half) & (u_id >= half)
        m2 = same_half & (t_id % half < SUB) & (u_id % half >= SUB)
        r1, r2a, r2b, last = half, SUB, half + SUB, 0
    else:
        m1 = (t_id >= half) & (u_id < half)
        m2 = same_half & (t_id % half >= SUB) & (u_id % half < SUB)
        r1, r2a, r2b, last = half - 1, SUB - 1, half + SUB - 1, L - 1
    ones = jnp.ones((HG_K_DIM, LANES), BF16)

    for h in range(heads_per_step):
        cols = slice(h * HG_K_DIM, (h + 1) * HG_K_DIM)
        lb = lb_ref[:, cols]
        q = _silu(q_ref[0][:, cols])
        v = _silu(i_ref[0][:, cols])
        f = lb + (1.0 - lb) * _sigmoid(f_ref[0][:, cols])
        kk = 1.0 - f
        G = _dot_f32(tri, jnp.log(f))
        a1 = G[r1:r1 + 1, :]
        a2 = jnp.where(row_id < half, G[r2a:r2a + 1, :], G[r2b:r2b + 1, :])
        A1 = _dot_nt(q * jnp.exp(jnp.minimum(G - a1, 0.0)), kk * jnp.exp(jnp.minimum(a1 - G, 0.0)))
        A2 = _dot_nt(q * jnp.exp(jnp.minimum(G - a2, 0.0)), kk * jnp.exp(jnp.minimum(a2 - G, 0.0)))
        a_off = jnp.where(m1, A1, jnp.where(m2, A2, 0.0))
        for blk in range(L // SUB):
            kb = kk[blk * SUB:(blk + 1) * SUB, :]
            gb = G[blk * SUB:(blk + 1) * SUB, :]
            for t in range(SUB):
                r = blk * SUB + t
                ok = (sub_id >= t) if rev else (sub_id <= t)
                dec = jnp.exp(jnp.where(ok, G[r:r + 1, :] - gb, NEG))
                p_ref[h, r * SUB:(r + 1) * SUB, :] = q[r:r + 1, :] * kb * dec
        rsum = _dot(p_ref[h], ones)
        for blk in range(L // SUB):
            vb = v[blk * SUB:(blk + 1) * SUB, :]
            for t in range(SUB):
                r = blk * SUB + t
                od_ref[h, r:r + 1, :] = jnp.sum(rsum[r * SUB:(r + 1) * SUB, :] * vb, axis=0, keepdims=True)
        st = s_ref[h]
        o = _dot(a_off, v) + od_ref[h] + _dot_nt(q * jnp.exp(G), st)
        gl = G[last:last + 1, :]
        s_ref[h] = st * jnp.exp(gl) + _dot_tn(v, kk * jnp.exp(gl - G))
        if rev:
            tot = o + prev_ref[0][:, cols]
            y = _rms(tot) * gain_ref[:, cols] * _silu(og_ref[0][:, cols])
            o_ref[0, :, cols] = y.astype(o_ref.dtype)
        else:
            o_ref[0, :, cols] = o


def hgrn2(proj, lower_bound, hg_gain, t_lat, heads_per_step=HG_HEADS):
    nb, t, _ = proj.shape
    L = HG_CHUNK
    n_lat, n_ctx = t_lat // L, (t - t_lat) // L
    width = heads_per_step * HG_K_DIM
    n_hb = HG_HEADS // heads_per_step

    def call(rev, prev):
        idx = lambda s: _chunk_index(s, n_lat, n_ctx, rev)
        ns = HG_SAMPLES_PER_STEP
        tok = lambda col: pl.BlockSpec((ns, L, width), lambda b, hb, s: (b, idx(s), col // width + hb))
        vec = pl.BlockSpec((1, width), lambda b, hb, s: (0, hb))
        in_specs = [tok(ODD_Q), tok(ODD_FB if rev else ODD_FF), tok(ODD_I), vec]
        args = [proj, proj, proj, lower_bound.reshape(1, -1)]
        per_sample = [True] * 3 + [False]
        if rev:
            in_specs += [tok(0), tok(ODD_G), vec]
            args += [prev, proj, hg_gain.reshape(1, -1)]
            per_sample += [True, True, False]
        return pl.pallas_call(
            _for_each_sample(functools.partial(_hgrn_kernel, rev=rev, heads_per_step=heads_per_step),
                             per_sample + [True], 3, step_axis=2),
            grid=(nb // ns, n_hb, n_lat + n_ctx),
            in_specs=in_specs,
            out_specs=tok(0),
            out_shape=jax.ShapeDtypeStruct((nb, t, HG_W), BF16 if rev else F32),
            scratch_shapes=[_sample_scratch(ns, (heads_per_step, HG_V_DIM, HG_K_DIM)),
                            _sample_scratch(ns, (heads_per_step, L * HG_SUB, HG_K_DIM)),
                            _sample_scratch(ns, (heads_per_step, L, HG_V_DIM))],
            compiler_params=_params(("parallel", "parallel", "arbitrary")),
            name="hgrn_bwd" if rev else "hgrn_fwd",
        )(*args)

    return call(True, call(False, None))


def _conv_kernel(x_ref, w_ref, b_ref, o_ref, *, t_lat):
    x = x_ref[0]
    n = x.shape[0]
    row = lax.broadcasted_iota(jnp.int32, (n, 1), 0)
    acc = jnp.zeros_like(x) + b_ref[...]
    for j in range(SSD_CONV):
        d = j - SSD_CONV // 2
        src = row + d
        ok = (src >= 0) & (src < n) & ((row < t_lat) == (src < t_lat))
        shifted = x if d == 0 else pltpu.roll(x, (-d) % n, 0)
        acc += jnp.where(ok, shifted, 0.0) * w_ref[j:j + 1, :]
    o_ref[0] = _silu(acc)


def ssd_conv(proj, conv_w, conv_b, t_lat):
    nb, t, _ = proj.shape
    return pl.pallas_call(
        functools.partial(_conv_kernel, t_lat=t_lat),
        grid=(nb, SSD_XBC // LANES),
        in_specs=[pl.BlockSpec((1, t, LANES), lambda b, j: (b, 0, ODD_XBC // LANES + j)),
                  pl.BlockSpec((SSD_CONV, LANES), lambda b, j: (0, j)),
                  pl.BlockSpec((1, LANES), lambda b, j: (0, j))],
        out_specs=pl.BlockSpec((1, t, LANES), lambda b, j: (b, 0, j)),
        out_shape=jax.ShapeDtypeStruct((nb, t, SSD_XBC), F32),
        compiler_params=_params(("parallel", "parallel")),
        name="ssd_conv",
    )(proj, conv_w, conv_b.reshape(1, -1))


def _ssd_kernel(*refs, rev):
    if rev:
        (x_ref, bc_ref, gc_ref, gr_ref, dbr_ref, dbc_ref, alr_ref, alc_ref, prev_ref, z_ref, skip_ref, gain_ref,
         o_ref, s_ref, xw_ref, y_ref) = refs
    else:
        x_ref, bc_ref, gc_ref, gr_ref, dbr_ref, dbc_ref, alr_ref, alc_ref, o_ref, s_ref, xw_ref = refs
    L = SSD_CHUNK
    hpg = SSD_HEADS // SSD_GROUPS
    gw = hpg * SSD_HEAD_DIM

    d0 = SSD_HEADS if rev else 0
    dt_col = _softplus(gc_ref[0][:, d0:d0 + SSD_HEADS] + dbr_ref[:, d0:d0 + SSD_HEADS])
    dt_row = _softplus(gr_ref[0, 0][d0:d0 + SSD_HEADS, :] + dbc_ref[d0:d0 + SSD_HEADS, :])
    a_col = dt_col * -jnp.exp(alr_ref[:, d0:d0 + SSD_HEADS])
    a_row = dt_row * -jnp.exp(alc_ref[d0:d0 + SSD_HEADS, :])
    mask = _time_tri(L, rev)
    tri = mask.astype(F32)
    ac_col = _dot_f32(tri, a_col)
    ac_row = _dot_nt_f32(a_row, tri)
    last = 0 if rev else L - 1
    x = x_ref[0]
    bc = bc_ref[0]

    for g in range(SSD_GROUPS):
        bm = bc[:, g * SSD_STATE:(g + 1) * SSD_STATE]
        cm = bc[:, SSD_BC_W + g * SSD_STATE:SSD_BC_W + (g + 1) * SSD_STATE]
        cb = _dot_nt(cm, bm)
        state = s_ref[g]
        inter = _dot_nt(cm, state)
        for hh in range(hpg):
            h = g * hpg + hh
            cols = slice(h * SSD_HEAD_DIM, (h + 1) * SSD_HEAD_DIM)
            gcols = slice(hh * SSD_HEAD_DIM, (hh + 1) * SSD_HEAD_DIM)
            acol, arow = ac_col[:, h:h + 1], ac_row[h:h + 1, :]
            decay = jnp.exp(jnp.where(mask, acol - arow, NEG))
            xh = x[:, cols]
            y = _dot(cb * decay * dt_row[h:h + 1, :], xh) + jnp.exp(acol) * inter[:, gcols]
            a_last = acol[last:last + 1, :]
            xw_ref[:, gcols] = xh * (jnp.exp(a_last - acol) * dt_col[:, h:h + 1])
            s_ref[g, gcols, :] = state[gcols, :] * jnp.exp(a_last)
            if rev:
                y_ref[:, cols] = y + prev_ref[0][:, cols]
            else:
                o_ref[0, :, cols] = y
        s_ref[g] += _dot_tn(xw_ref[...], bm)

    if rev:
        yt = (y_ref[...] + skip_ref[...] * x) * _silu(z_ref[0])
        for g in range(SSD_GROUPS):
            cols = slice(g * gw, (g + 1) * gw)
            o_ref[0, :, cols] = (_rms(yt[:, cols]) * gain_ref[:, cols]).astype(o_ref.dtype)


def ssd(proj, gates, xbc, dt_bias, a_log, d_skip, ssd_gain, t_lat):
    nb, t, _ = proj.shape
    L = SSD_CHUNK
    n_lat, n_ctx = t_lat // L, (t - t_lat) // L
    gate_rows = _gate_rows(gates, L)
    skip = jnp.repeat(d_skip.astype(F32), SSD_HEAD_DIM).reshape(1, SSD_W)

    def call(rev, prev):
        idx = lambda b, s: _chunk_index(s, n_lat, n_ctx, rev)
        ns = SSD_SAMPLES_PER_STEP
        tok = lambda width, col: pl.BlockSpec((ns, L, width), lambda b, s: (b, idx(b, s), col // width))
        const = lambda shape: pl.BlockSpec(shape, lambda b, s: (0, 0))
        in_specs = [tok(SSD_W, 0), tok(2 * SSD_BC_W, SSD_W),
                    pl.BlockSpec((ns, L, LANES), lambda b, s: (b, idx(b, s), 0)),
                    pl.BlockSpec((ns, 1, GATE_ROWS, L), lambda b, s: (b, idx(b, s), 0, 0)),
                    const((1, LANES)), const((GATE_ROWS, 1)), const((1, LANES)), const((GATE_ROWS, 1))]
        args = [xbc, xbc, gates, gate_rows, _lane_row(dt_bias), _sublane_col(dt_bias),
                _lane_row(a_log), _sublane_col(a_log)]
        per_sample = [True] * 4 + [False] * 4
        scratch = [_sample_scratch(ns, (SSD_GROUPS, SSD_W // SSD_GROUPS, SSD_STATE)),
                   _sample_scratch(ns, (L, SSD_W // SSD_GROUPS))]
        if rev:
            in_specs += [tok(SSD_W, 0), tok(SSD_W, ODD_Z), const((1, SSD_W)), const((1, SSD_W))]
            args += [prev, proj, skip, ssd_gain.reshape(1, -1)]
            per_sample += [True, True, False, False]
            scratch += [_sample_scratch(ns, (L, SSD_W))]
        return pl.pallas_call(
            _for_each_sample(functools.partial(_ssd_kernel, rev=rev), per_sample + [True], len(scratch),
                             step_axis=1),
            grid=(nb // ns, n_lat + n_ctx),
            in_specs=in_specs,
            out_specs=tok(SSD_W, 0),
            out_shape=jax.ShapeDtypeStruct((nb, t, SSD_W), BF16 if rev else F32),
            scratch_shapes=scratch,
            compiler_params=_params(("parallel", "arbitrary")),
            name="ssd_bwd" if rev else "ssd_fwd",
        )(*args)

    return call(True, call(False, None))


def _split_w_in(w_in, n_gate):
    n_main = w_in.shape[1] - n_gate
    w_gate = jnp.zeros((w_in.shape[0], LANES), F32).at[:, :n_gate].set(w_in[:, n_main:])
    return w_in[:, :n_main].astype(BF16), w_gate


def kernel(x, c, ctx, c_ctx, hgrn_lb_logits,
           l0_mod_w, l0_mod_b, l0_w_in, l0_q_gain, l0_k_gain, l0_rpb, l0_gate_bias, l0_ml_gain,
           l0_w_out, l0_ffn_w1, l0_ffn_w3, l0_ffn_w2,
           l1_mod_w, l1_mod_b, l1_w_in, l1_hg_gain, l1_conv_w, l1_conv_b, l1_dt_bias, l1_a_log,
           l1_d_skip, l1_ssd_gain, l1_w_out, l1_ffn_w1, l1_ffn_w3, l1_ffn_w2):
    nb, t_lat, d = x.shape
    p = jax.nn.softmax(hgrn_lb_logits.astype(F32), axis=0)
    lower_bounds = jnp.cumsum(p, axis=0) - p[0]

    cvecs = jnp.zeros((8, d), F32).at[:nb].set(c).at[nb].set(c_ctx)
    xs = jnp.concatenate([x, ctx], axis=1)

    layers = [
        (l0_mod_w, l0_mod_b, l0_w_in, 4 * ML_HEADS, l0_w_out, l0_ffn_w1, l0_ffn_w3, l0_ffn_w2),
        (l1_mod_w, l1_mod_b, l1_w_in, 2 * SSD_HEADS, l1_w_out, l1_ffn_w1, l1_ffn_w3, l1_ffn_w2),
    ]
    for layer, (mod_w, mod_b, w_in, n_gate, w_out, w1, w3, w2) in enumerate(layers):
        mods = modulation_vectors(cvecs, mod_w, mod_b).reshape(8, N_MOD, 1, d)
        sh1, sc1, g1, sh2, sc2, g2 = [mods[:nb + 1, i] for i in range(N_MOD)]
        w_main, w_gate = _split_w_in(w_in, n_gate)
        proj, gates = modulated_in_proj(xs, sh1, sc1, w_main, w_gate, t_lat)
        if layer == 0:
            bias = na_bias_tables(l0_rpb, t_lat // GRID_W)
            ya = neighbourhood_attention(proj, bias, l0_q_gain, l0_k_gain, t_lat)
            yb = mlstm(proj, gates, l0_gate_bias, l0_ml_gain, t_lat)
        else:
            ya = hgrn2(proj, lower_bounds[layer], l1_hg_gain, t_lat)
            xbc = ssd_conv(proj, l1_conv_w, l1_conv_b, t_lat)
            yb = ssd(proj, gates, xbc, l1_dt_bias, l1_a_log, l1_d_skip, l1_ssd_gain, t_lat)
        ka = ya.shape[-1]
        t_out = xs.shape[1] if layer == 0 else t_lat
        xs = out_proj_residual(ya, yb, xs, g1, w_out[:ka].astype(BF16), w_out[ka:].astype(BF16), t_lat, t_out)
        xs = modulated_ffn_residual(xs, sh2, sc2, g2, w1.astype(BF16), w3.astype(BF16), w2.astype(BF16), t_lat)
    return xs
```

```python
import functools

import jax
import jax.numpy as jnp
import numpy as np
from jax import lax
from jax.experimental import pallas as pl
from jax.experimental.pallas import tpu as pltpu

EPS = 1e-6
N_MOD = 6
GRID_W = 64

NA_HEADS = 8
NA_HEAD_DIM = 128
NA_WIN_H = 8
NA_WIN_W = 16
NA_GROUP_ROWS = 4
NA_BAND_ROWS = NA_GROUP_ROWS + NA_WIN_H - 1
NA_GROUP = NA_GROUP_ROWS * GRID_W
NA_BAND = NA_BAND_ROWS * GRID_W

ML_HEADS = 4
ML_QK_DIM = 128
ML_V_DIM = 256
ML_CHUNK = 64

HG_HEADS = 8
HG_K_DIM = 128
HG_V_DIM = 128
HG_CHUNK = 32
HG_SUB = 8

SSD_HEADS = 16
SSD_HEAD_DIM = 64
SSD_GROUPS = 2
SSD_STATE = 128
SSD_CONV = 5
SSD_CHUNK = 64

NA_W = NA_HEADS * NA_HEAD_DIM
ML_QK_W = ML_HEADS * ML_QK_DIM
ML_W = ML_HEADS * ML_V_DIM
HG_K_W = HG_HEADS * HG_K_DIM
HG_W = HG_HEADS * HG_V_DIM
SSD_W = SSD_HEADS * SSD_HEAD_DIM
SSD_BC_W = SSD_GROUPS * SSD_STATE
SSD_XBC = SSD_W + 2 * SSD_BC_W

EVEN_Q, EVEN_K, EVEN_V = 0, NA_W, 2 * NA_W
EVEN_MQ = 3 * NA_W
EVEN_MK = EVEN_MQ + ML_QK_W
EVEN_MV = EVEN_MK + ML_QK_W
EVEN_MO = EVEN_MV + ML_W
ODD_Q, ODD_FF, ODD_FB = 0, HG_K_W, 2 * HG_K_W
ODD_I = 3 * HG_K_W
ODD_G = ODD_I + HG_W
ODD_Z = ODD_G + HG_W
ODD_XBC = ODD_Z + SSD_W

VMEM_LIMIT_BYTES = 56 * 1024 * 1024
LANES = 128
GATE_ROWS = 32
NEG = -1e30
ML_SAMPLES_PER_STEP = 4
HG_SAMPLES_PER_STEP = 4
SSD_SAMPLES_PER_STEP = 1

F32 = jnp.float32
BF16 = jnp.bfloat16
HIGHEST = lax.Precision.HIGHEST


def _params(sem):
    return pltpu.CompilerParams(dimension_semantics=sem, vmem_limit_bytes=VMEM_LIMIT_BYTES)


def _dot(a, b):
    return jnp.dot(a.astype(BF16), b.astype(BF16), preferred_element_type=F32)


def _dot_nt(a, b):
    return lax.dot_general(a.astype(BF16), b.astype(BF16), (((1,), (1,)), ((), ())), preferred_element_type=F32)


def _dot_tn(a, b):
    return lax.dot_general(a.astype(BF16), b.astype(BF16), (((0,), (0,)), ((), ())), preferred_element_type=F32)


def _dot_f32(a, b):
    return jnp.dot(a, b, precision=HIGHEST, preferred_element_type=F32)


def _dot_nt_f32(a, b):
    return lax.dot_general(a, b, (((1,), (1,)), ((), ())), precision=HIGHEST, preferred_element_type=F32)


def _sigmoid(x):
    return 1.0 / (1.0 + jnp.exp(-x))


def _silu(x):
    return x * _sigmoid(x)


def _log_sigmoid(x):
    return jnp.minimum(x, 0.0) - jnp.log(1.0 + jnp.exp(-jnp.abs(x)))


def _softplus(x):
    return jnp.maximum(x, 0.0) + jnp.log(1.0 + jnp.exp(-jnp.abs(x)))


def _rms(x):
    return x * lax.rsqrt(jnp.mean(x * x, axis=-1, keepdims=True) + EPS)


def _time_tri(n, rev):
    t = lax.broadcasted_iota(jnp.int32, (n, n), 0)
    u = lax.broadcasted_iota(jnp.int32, (n, n), 1)
    return (u >= t) if rev else (u <= t)


def _chunk_index(step, n_lat, n_ctx, rev):
    if rev:
        return n_lat + n_ctx - 1 - step
    return jnp.where(step < n_ctx, n_lat + step, step - n_ctx)


def _mod_kernel(c_ref, w_ref, b_ref, o_ref):
    o_ref[...] = jnp.dot(_silu(c_ref[...]), w_ref[...], preferred_element_type=F32) + b_ref[...]


def modulation_vectors(cvecs, mod_w, mod_b, tn=1024):
    r, d = cvecs.shape
    n = mod_w.shape[1]
    return pl.pallas_call(
        _mod_kernel,
        grid=(n // tn,),
        in_specs=[pl.BlockSpec((r, d), lambda j: (0, 0)),
                  pl.BlockSpec((d, tn), lambda j: (0, j)),
                  pl.BlockSpec((1, tn), lambda j: (0, j))],
        out_specs=pl.BlockSpec((r, tn), lambda j: (0, j)),
        out_shape=jax.ShapeDtypeStruct((r, n), F32),
        compiler_params=_params(("arbitrary",)),
        name="modulation",
    )(cvecs, mod_w, mod_b.reshape(1, n))


def _mod_map(n_lat_blocks, n_batch):
    return lambda b, i, j: (jnp.where(i >= n_lat_blocks, n_batch, b), 0, 0)


def _inproj_kernel(x_ref, sh_ref, sc_ref, w_ref, wg_ref, o_ref, og_ref, h_ref):
    @pl.when(pl.program_id(2) == 0)
    def _():
        h = _rms(x_ref[0]) * (1.0 + sc_ref[0]) + sh_ref[0]
        h_ref[...] = h.astype(BF16)
        og_ref[0] = jnp.dot(h, wg_ref[...], preferred_element_type=F32)

    o_ref[0] = jnp.dot(h_ref[...], w_ref[...], preferred_element_type=F32)


def modulated_in_proj(x, shift, scale, w_main, w_gate, t_lat, tm=512, tn=1536):
    nb, t, d = x.shape
    n = w_main.shape[1]
    mod_map = _mod_map(t_lat // tm, nb)
    return pl.pallas_call(
        _inproj_kernel,
        grid=(nb, pl.cdiv(t, tm), n // tn),
        in_specs=[pl.BlockSpec((1, tm, d), lambda b, i, j: (b, i, 0)),
                  pl.BlockSpec((1, 1, d), mod_map),
                  pl.BlockSpec((1, 1, d), mod_map),
                  pl.BlockSpec((d, tn), lambda b, i, j: (0, j)),
                  pl.BlockSpec((d, LANES), lambda b, i, j: (0, 0))],
        out_specs=[pl.BlockSpec((1, tm, tn), lambda b, i, j: (b, i, j)),
                   pl.BlockSpec((1, tm, LANES), lambda b, i, j: (b, i, 0))],
        out_shape=[jax.ShapeDtypeStruct((nb, t, n), F32),
                   jax.ShapeDtypeStruct((nb, t, LANES), F32)],
        scratch_shapes=[pltpu.VMEM((tm, d), BF16)],
        compiler_params=_params(("parallel", "parallel", "arbitrary")),
        name="in_proj",
    )(x, shift, scale, w_main, w_gate)


def _outproj_kernel(ya_ref, yb_ref, x_ref, g_ref, wa_ref, wb_ref, o_ref):
    acc = jnp.dot(ya_ref[0], wa_ref[...], preferred_element_type=F32)
    acc += jnp.dot(yb_ref[0], wb_ref[...], preferred_element_type=F32)
    o_ref[0] = x_ref[0] + g_ref[0] * acc


def out_proj_residual(ya, yb, x, gate, w_a, w_b, t_lat, t_out, tm=512, tn=2048):
    nb, _, k = ya.shape
    n = w_a.shape[1]
    mod_map = _mod_map(t_lat // tm, nb)
    return pl.pallas_call(
        _outproj_kernel,
        grid=(nb, pl.cdiv(t_out, tm), n // tn),
        in_specs=[pl.BlockSpec((1, tm, k), lambda b, i, j: (b, i, 0)),
                  pl.BlockSpec((1, tm, k), lambda b, i, j: (b, i, 0)),
                  pl.BlockSpec((1, tm, tn), lambda b, i, j: (b, i, j)),
                  pl.BlockSpec((1, 1, tn), lambda b, i, j: mod_map(b, i, j)[:2] + (j,)),
                  pl.BlockSpec((k, tn), lambda b, i, j: (0, j)),
                  pl.BlockSpec((k, tn), lambda b, i, j: (0, j))],
        out_specs=pl.BlockSpec((1, tm, tn), lambda b, i, j: (b, i, j)),
        out_shape=jax.ShapeDtypeStruct((nb, t_out, n), F32),
        compiler_params=_params(("parallel", "parallel", "arbitrary")),
        name="out_proj",
    )(ya, yb, x, gate, w_a, w_b)


def _ffn_kernel(x_ref, sh_ref, sc_ref, g_ref, w1_ref, w3_ref, w2_ref, o_ref, h_ref, acc_ref):
    j = pl.program_id(2)

    @pl.when(j == 0)
    def _():
        h_ref[...] = (_rms(x_ref[0]) * (1.0 + sc_ref[0]) + sh_ref[0]).astype(BF16)
        acc_ref[...] = jnp.zeros_like(acc_ref)

    h = h_ref[...]
    a = jnp.dot(h, w1_ref[...], preferred_element_type=F32)
    b = jnp.dot(h, w3_ref[...], preferred_element_type=F32)
    acc_ref[...] += jnp.dot((_silu(a) * b).astype(BF16), w2_ref[...], preferred_element_type=F32)

    @pl.when(j == pl.num_programs(2) - 1)
    def _():
        o_ref[0] = x_ref[0] + g_ref[0] * acc_ref[...]


def modulated_ffn_residual(x, shift, scale, gate, w1, w3, w2, t_lat, tm=512, th=512):
    nb, t, d = x.shape
    hid = w1.shape[1]
    mod_map = _mod_map(t_lat // tm, nb)
    return pl.pallas_call(
        _ffn_kernel,
        grid=(nb, pl.cdiv(t, tm), hid // th),
        in_specs=[pl.BlockSpec((1, tm, d), lambda b, i, j: (b, i, 0)),
                  pl.BlockSpec((1, 1, d), mod_map),
                  pl.BlockSpec((1, 1, d), mod_map),
                  pl.BlockSpec((1, 1, d), mod_map),
                  pl.BlockSpec((d, th), lambda b, i, j: (0, j)),
                  pl.BlockSpec((d, th), lambda b, i, j: (0, j)),
                  pl.BlockSpec((th, d), lambda b, i, j: (j, 0))],
        out_specs=pl.BlockSpec((1, tm, d), lambda b, i, j: (b, i, 0)),
        out_shape=jax.ShapeDtypeStruct((nb, t, d), F32),
        scratch_shapes=[pltpu.VMEM((tm, d), BF16), pltpu.VMEM((tm, d), F32)],
        compiler_params=_params(("parallel", "parallel", "arbitrary")),
        name="ffn",
    )(x, shift, scale, gate, w1, w3, w2)


def _na_geometry(rows):
    n_groups = rows // NA_GROUP_ROWS
    band0 = np.clip(np.arange(n_groups) * NA_GROUP_ROWS - NA_WIN_H // 2, 0, rows - NA_BAND_ROWS)
    return n_groups, band0


def na_bias_tables(rpb, rows):
    n_groups, band0 = _na_geometry(rows)
    tables = []
    for g in (0, 1, n_groups - 1):
        qr = g * NA_GROUP_ROWS + np.arange(NA_GROUP_ROWS)
        r0 = np.clip(qr - NA_WIN_H // 2, 0, rows - NA_WIN_H)
        kr = band0[g] + np.arange(NA_BAND_ROWS)
        qc = np.arange(GRID_W)
        kc = np.arange(GRID_W)
        w0 = np.clip(qc - NA_WIN_W // 2, 0, GRID_W - NA_WIN_W)
        row_ok = (kr[None, :] >= r0[:, None]) & (kr[None, :] < r0[:, None] + NA_WIN_H)
        col_ok = (kc[None, :] >= w0[:, None]) & (kc[None, :] < w0[:, None] + NA_WIN_W)
        drow = np.clip(kr[None, :] - qr[:, None] + NA_WIN_H - 1, 0, 2 * NA_WIN_H - 2)
        dcol = np.clip(kc[None, :] - qc[:, None], -(NA_WIN_W - 1), NA_WIN_W - 1) + NA_WIN_W - 1
        ok = row_ok[:, None, :, None] & col_ok[None, :, None, :]
        row_sel = np.eye(2 * NA_WIN_H - 1, dtype=np.float32)[drow]
        col_sel = np.eye(2 * NA_WIN_W - 1, dtype=np.float32)[dcol]
        by_row = jnp.einsum('hab,rka->hrkb', rpb.astype(F32), row_sel, precision=HIGHEST)
        bias = jnp.einsum('hrkb,cdb->hrckd', by_row, col_sel, precision=HIGHEST)
        tables.append(jnp.where(jnp.asarray(ok), bias, NEG).reshape(NA_HEADS, NA_GROUP, NA_BAND))
    return jnp.stack(tables)


def _na_kernel(q_ref, k_ref, v_ref, bias_ref, qg_ref, kg_ref, o_ref, *, rows, t_lat):
    g = pl.program_id(2)
    n_groups = rows // NA_GROUP_ROWS
    qn = (_rms(q_ref[0]) * qg_ref[...] * NA_HEAD_DIM ** -0.5).astype(BF16)
    kc = (_rms(k_ref[0, t_lat:, :]) * kg_ref[...]).astype(BF16)
    vc = v_ref[0, t_lat:, :].astype(BF16)
    s_ctx = _dot_nt(qn, kc)

    @pl.when(g < n_groups)
    def _():
        band0 = jnp.clip(g * NA_GROUP_ROWS - NA_WIN_H // 2, 0, rows - NA_BAND_ROWS)
        start = pl.multiple_of(band0 * GRID_W, GRID_W)
        kb = (_rms(k_ref[0, pl.ds(start, NA_BAND), :]) * kg_ref[...]).astype(BF16)
        vb = v_ref[0, pl.ds(start, NA_BAND), :].astype(BF16)
        s_loc = _dot_nt(qn, kb) + bias_ref[0, 0]
        m = jnp.maximum(jnp.max(s_loc, axis=-1, keepdims=True), jnp.max(s_ctx, axis=-1, keepdims=True))
        p_loc = jnp.exp(s_loc - m)
        p_ctx = jnp.exp(s_ctx - m)
        den = jnp.sum(p_loc, axis=-1, keepdims=True) + jnp.sum(p_ctx, axis=-1, keepdims=True)
        o_ref[0] = ((_dot(p_loc, vb) + _dot(p_ctx, vc)) / den).astype(o_ref.dtype)

    @pl.when(g >= n_groups)
    def _():
        p = jnp.exp(s_ctx - jnp.max(s_ctx, axis=-1, keepdims=True))
        o_ref[0] = (_dot(p, vc) / jnp.sum(p, axis=-1, keepdims=True)).astype(o_ref.dtype)


def neighbourhood_attention(proj, bias, q_gain, k_gain, t_lat):
    nb, t, _ = proj.shape
    rows = t_lat // GRID_W
    n_groups, _ = _na_geometry(rows)
    assert t - t_lat == NA_GROUP
    table = lambda b, h, g: (jnp.where(g == 0, 0, jnp.where(g >= n_groups - 1, 2, 1)), h, 0, 0)
    return pl.pallas_call(
        functools.partial(_na_kernel, rows=rows, t_lat=t_lat),
        grid=(nb, NA_HEADS, n_groups + 1),
        in_specs=[pl.BlockSpec((1, NA_GROUP, NA_HEAD_DIM), lambda b, h, g: (b, g, EVEN_Q // NA_HEAD_DIM + h)),
                  pl.BlockSpec((1, t, NA_HEAD_DIM), lambda b, h, g: (b, 0, EVEN_K // NA_HEAD_DIM + h)),
                  pl.BlockSpec((1, t, NA_HEAD_DIM), lambda b, h, g: (b, 0, EVEN_V // NA_HEAD_DIM + h)),
                  pl.BlockSpec((1, 1, NA_GROUP, NA_BAND), table),
                  pl.BlockSpec((1, NA_HEAD_DIM), lambda b, h, g: (0, 0)),
                  pl.BlockSpec((1, NA_HEAD_DIM), lambda b, h, g: (0, 0))],
        out_specs=pl.BlockSpec((1, NA_GROUP, NA_HEAD_DIM), lambda b, h, g: (b, g, h)),
        out_shape=jax.ShapeDtypeStruct((nb, t, NA_W), BF16),
        compiler_params=_params(("parallel", "parallel", "arbitrary")),
        name="natten",
    )(proj, proj, proj, bias, q_gain.reshape(1, -1), k_gain.reshape(1, -1))


def _mlstm_kernel(*refs, rev):
    if rev:
        (q_ref, k_ref, v_ref, gc_ref, gr_ref, gbr_ref, gbc_ref, prev_ref, og_ref, gain_ref,
         o_ref, c_ref, n_ref, m_ref) = refs
    else:
        q_ref, k_ref, v_ref, gc_ref, gr_ref, gbr_ref, gbc_ref, o_ref, c_ref, n_ref, m_ref = refs
    L = ML_CHUNK

    base = 2 * ML_HEADS if rev else 0
    gcol = gc_ref[0][:, :4 * ML_HEADS] + gbr_ref[:, :4 * ML_HEADS]
    grow = gr_ref[0, 0][:4 * ML_HEADS, :] + gbc_ref[:4 * ML_HEADS, :]
    li_col = gcol[:, base:base + ML_HEADS]
    lf_col = _log_sigmoid(gcol[:, base + ML_HEADS:base + 2 * ML_HEADS])
    li_row = grow[base:base + ML_HEADS, :]
    lf_row = _log_sigmoid(grow[base + ML_HEADS:base + 2 * ML_HEADS, :])
    mask = _time_tri(L, rev)
    tri = mask.astype(F32)
    bc_col = _dot_f32(tri, lf_col)
    bc_row = _dot_nt_f32(lf_row, tri)
    last = 0 if rev else L - 1

    for h in range(ML_HEADS):
        bcol, brow = bc_col[:, h:h + 1], bc_row[h:h + 1, :]
        licol, lirow = li_col[:, h:h + 1], li_row[h:h + 1, :]
        m_prev = m_ref[h]
        q = q_ref[0][:, h * ML_QK_DIM:(h + 1) * ML_QK_DIM] * ML_QK_DIM ** -0.5
        k = k_ref[0][:, h * ML_QK_DIM:(h + 1) * ML_QK_DIM]
        v = v_ref[0][:, h * ML_V_DIM:(h + 1) * ML_V_DIM]
        dlog = jnp.where(mask, bcol - brow + lirow, NEG)
        inter = bcol + m_prev
        m_t = jnp.maximum(inter, jnp.max(dlog, axis=-1, keepdims=True))
        w_inter = jnp.exp(inter - m_t)
        s = _dot_nt(q, k) * jnp.exp(dlog - m_t)
        c_state, n_state = c_ref[h], n_ref[h]
        num = w_inter * _dot_nt(q, c_state) + _dot(s, v)
        den = w_inter * jnp.sum(q * n_state, axis=-1, keepdims=True) + jnp.sum(s, axis=-1, keepdims=True)
        hid = num / jnp.maximum(jnp.abs(den), jnp.exp(-m_t))
        b_last = bcol[last:last + 1, :]
        gg = b_last - bcol + licol
        m_new = jnp.maximum(b_last + m_prev, jnp.max(gg, axis=0, keepdims=True))
        decay = jnp.exp(b_last + m_prev - m_new)
        w = jnp.exp(gg - m_new)
        c_ref[h] = decay * c_state + _dot_tn(v * w, k)
        n_ref[h] = decay * n_state + jnp.sum(k * w, axis=0, keepdims=True)
        m_ref[h] = m_new
        cols = slice(h * ML_V_DIM, (h + 1) * ML_V_DIM)
        if rev:
            tot = hid + prev_ref[0][:, cols]
            y = _rms(tot) * gain_ref[:, cols] * _sigmoid(og_ref[0][:, cols])
            o_ref[0, :, cols] = y.astype(o_ref.dtype)
        else:
            o_ref[0, :, cols] = hid


def _for_each_sample(body, per_sample, n_scratch, step_axis):
    def kern(*refs):
        n_io = len(refs) - n_scratch

        @pl.when(pl.program_id(step_axis) == 0)
        def _():
            for r in refs[n_io:]:
                r[...] = jnp.zeros_like(r)

        for bb in range(refs[-1].shape[0]):
            views = [r.at[bb:bb + 1] if per_sample[i] else r for i, r in enumerate(refs[:n_io])]
            body(*views, *[r.at[bb] for r in refs[n_io:]])
    return kern


def _sample_scratch(n_samples, shape):
    return pltpu.VMEM((n_samples,) + shape, F32)


def _gate_rows(gates, chunk):
    nb, t, _ = gates.shape
    return gates[..., :GATE_ROWS].reshape(nb, t // chunk, chunk, GATE_ROWS).transpose(0, 1, 3, 2)


def _lane_row(v):
    return jnp.zeros((1, LANES), F32).at[0, :v.size].set(v.reshape(-1).astype(F32))


def _sublane_col(v):
    return jnp.zeros((GATE_ROWS, 1), F32).at[:v.size, 0].set(v.reshape(-1).astype(F32))


def mlstm(proj, gates, gate_bias, ml_gain, t_lat):
    nb, t, _ = proj.shape
    L = ML_CHUNK
    n_lat, n_ctx = t_lat // L, (t - t_lat) // L
    gate_rows = _gate_rows(gates, L)
    gb_row, gb_col = _lane_row(gate_bias), _sublane_col(gate_bias)

    def call(rev, prev):
        idx = lambda b, s: _chunk_index(s, n_lat, n_ctx, rev)
        ns = ML_SAMPLES_PER_STEP
        tok = lambda width, col: pl.BlockSpec((ns, L, width), lambda b, s: (b, idx(b, s), col // width))
        in_specs = [tok(ML_QK_W, EVEN_MQ), tok(ML_QK_W, EVEN_MK), tok(ML_W, EVEN_MV),
                    pl.BlockSpec((ns, L, LANES), lambda b, s: (b, idx(b, s), 0)),
                    pl.BlockSpec((ns, 1, GATE_ROWS, L), lambda b, s: (b, idx(b, s), 0, 0)),
                    pl.BlockSpec((1, LANES), lambda b, s: (0, 0)),
                    pl.BlockSpec((GATE_ROWS, 1), lambda b, s: (0, 0))]
        args = [proj, proj, proj, gates, gate_rows, gb_row, gb_col]
        per_sample = [True] * 5 + [False] * 2
        if rev:
            in_specs += [tok(ML_W, 0), tok(ML_W, EVEN_MO), pl.BlockSpec((1, ML_W), lambda b, s: (0, 0))]
            args += [prev, proj, ml_gain.reshape(1, -1)]
            per_sample += [True, True, False]
        return pl.pallas_call(
            _for_each_sample(functools.partial(_mlstm_kernel, rev=rev), per_sample + [True], 3, step_axis=1),
            grid=(nb // ns, n_lat + n_ctx),
            in_specs=in_specs,
            out_specs=tok(ML_W, 0),
            out_shape=jax.ShapeDtypeStruct((nb, t, ML_W), BF16 if rev else F32),
            scratch_shapes=[_sample_scratch(ns, (ML_HEADS, ML_V_DIM, ML_QK_DIM)),
                            _sample_scratch(ns, (ML_HEADS, 1, ML_QK_DIM)),
                            _sample_scratch(ns, (ML_HEADS, 1, 1))],
            compiler_params=_params(("parallel", "arbitrary")),
            name="mlstm_bwd" if rev else "mlstm_fwd",
        )(*args)

    return call(True, call(False, None))


def _hgrn_kernel(*refs, rev, heads_per_step):
    if rev:
        q_ref, f_ref, i_ref, lb_ref, prev_ref, og_ref, gain_ref, o_ref, s_ref, p_ref, od_ref = refs
    else:
        q_ref, f_ref, i_ref, lb_ref, o_ref, s_ref, p_ref, od_ref = refs
    L, SUB = HG_CHUNK, HG_SUB
    half = L // 2

    mask = _time_tri(L, rev)
    tri = mask.astype(F32)
    t_id = lax.broadcasted_iota(jnp.int32, (L, L), 0)
    u_id = lax.broadcasted_iota(jnp.int32, (L, L), 1)
    row_id = lax.broadcasted_iota(jnp.int32, (L, 1), 0)
    sub_id = lax.broadcasted_iota(jnp.int32, (SUB, 1), 0)
    same_half = (t_id // half) == (u_id // half)
    if rev:
        m1 = (t_id < half) & (u_id >= half)
        m2 = same_half & (t_id % half < SUB) & (u_id % half >= SUB)
        r1, r2a, r2b, last = half, SUB, half + SUB, 0
    else:
        m1 = (t_id >= half) & (u_id < half)
        m2 = same_half & (t_id % half >= SUB) & (u_id % half < SUB)
        r1, r2a, r2b, last = half - 1, SUB - 1, half + SUB - 1, L - 1
    ones = jnp.ones((HG_K_DIM, LANES), BF16)

    for h in range(heads_per_step):
        cols = slice(h * HG_K_DIM, (h + 1) * HG_K_DIM)
        lb = lb_ref[:, cols]
        q = _silu(q_ref[0][:, cols])
        v = _silu(i_ref[0][:, cols])
        f = lb + (1.0 - lb) * _sigmoid(f_ref[0][:, cols])
        kk = 1.0 - f
        G = _dot_f32(tri, jnp.log(f))
        a1 = G[r1:r1 + 1, :]
        a2 = jnp.where(row_id < half, G[r2a:r2a + 1, :], G[r2b:r2b + 1, :])
        A1 = _dot_nt(q * jnp.exp(jnp.minimum(G - a1, 0.0)), kk * jnp.exp(jnp.minimum(a1 - G, 0.0)))
        A2 = _dot_nt(q * jnp.exp(jnp.minimum(G - a2, 0.0)), kk * jnp.exp(jnp.minimum(a2 - G, 0.0)))
        a_off = jnp.where(m1, A1, jnp.where(m2, A2, 0.0))
        for blk in range(L // SUB):
            kb = kk[blk * SUB:(blk + 1) * SUB, :]
            gb = G[blk * SUB:(blk + 1) * SUB, :]
            for t in range(SUB):
                r = blk * SUB + t
                ok = (sub_id >= t) if rev else (sub_id <= t)
                dec = jnp.exp(jnp.where(ok, G[r:r + 1, :] - gb, NEG))
                p_ref[h, r * SUB:(r + 1) * SUB, :] = q[r:r + 1, :] * kb * dec
        rsum = _dot(p_ref[h], ones)
        for blk in range(L // SUB):
            vb = v[blk * SUB:(blk + 1) * SUB, :]
            for t in range(SUB):
                r = blk * SUB + t
                od_ref[h, r:r + 1, :] = jnp.sum(rsum[r * SUB:(r + 1) * SUB, :] * vb, axis=0, keepdims=True)
        st = s_ref[h]
        o = _dot(a_off, v) + od_ref[h] + _dot_nt(q * jnp.exp(G), st)
        gl = G[last:last + 1, :]
        s_ref[h] = st * jnp.exp(gl) + _dot_tn(v, kk * jnp.exp(gl - G))
        if rev:
            tot = o + prev_ref[0][:, cols]
            y = _rms(tot) * gain_ref[:, cols] * _silu(og_ref[0][:, cols])
            o_ref[0, :, cols] = y.astype(o_ref.dtype)
        else:
            o_ref[0, :, cols] = o


def hgrn2(proj, lower_bound, hg_gain, t_lat, heads_per_step=HG_HEADS):
    nb, t, _ = proj.shape
    L = HG_CHUNK
    n_lat, n_ctx = t_lat // L, (t - t_lat) // L
    width = heads_per_step * HG_K_DIM
    n_hb = HG_HEADS // heads_per_step

    def call(rev, prev):
        idx = lambda s: _chunk_index(s, n_lat, n_ctx, rev)
        ns = HG_SAMPLES_PER_STEP
        tok = lambda col: pl.BlockSpec((ns, L, width), lambda b, hb, s: (b, idx(s), col // width + hb))
        vec = pl.BlockSpec((1, width), lambda b, hb, s: (0, hb))
        in_specs = [tok(ODD_Q), tok(ODD_FB if rev else ODD_FF), tok(ODD_I), vec]
        args = [proj, proj, proj, lower_bound.reshape(1, -1)]
        per_sample = [True] * 3 + [False]
        if rev:
            in_specs += [tok(0), tok(ODD_G), vec]
            args += [prev, proj, hg_gain.reshape(1, -1)]
            per_sample += [True, True, False]
        return pl.pallas_call(
            _for_each_sample(functools.partial(_hgrn_kernel, rev=rev, heads_per_step=heads_per_step),
                             per_sample + [True], 3, step_axis=2),
            grid=(nb // ns, n_hb, n_lat + n_ctx),
            in_specs=in_specs,
            out_specs=tok(0),
            out_shape=jax.ShapeDtypeStruct((nb, t, HG_W), BF16 if rev else F32),
            scratch_shapes=[_sample_scratch(ns, (heads_per_step, HG_V_DIM, HG_K_DIM)),
                            _sample_scratch(ns, (heads_per_step, L * HG_SUB, HG_K_DIM)),
                            _sample_scratch(ns, (heads_per_step, L, HG_V_DIM))],
            compiler_params=_params(("parallel", "parallel", "arbitrary")),
            name="hgrn_bwd" if rev else "hgrn_fwd",
        )(*args)

    return call(True, call(False, None))


def _conv_kernel(x_ref, w_ref, b_ref, o_ref, *, t_lat):
    x = x_ref[0]
    n = x.shape[0]
    row = lax.broadcasted_iota(jnp.int32, (n, 1), 0)
    acc = jnp.zeros_like(x) + b_ref[...]
    for j in range(SSD_CONV):
        d = j - SSD_CONV // 2
        src = row + d
        ok = (src >= 0) & (src < n) & ((row < t_lat) == (src < t_lat))
        shifted = x if d == 0 else pltpu.roll(x, (-d) % n, 0)
        acc += jnp.where(ok, shifted, 0.0) * w_ref[j:j + 1, :]
    o_ref[0] = _silu(acc)


def ssd_conv(proj, conv_w, conv_b, t_lat):
    nb, t, _ = proj.shape
    return pl.pallas_call(
        functools.partial(_conv_kernel, t_lat=t_lat),
        grid=(nb, SSD_XBC // LANES),
        in_specs=[pl.BlockSpec((1, t, LANES), lambda b, j: (b, 0, ODD_XBC // LANES + j)),
                  pl.BlockSpec((SSD_CONV, LANES), lambda b, j: (0, j)),
                  pl.BlockSpec((1, LANES), lambda b, j: (0, j))],
        out_specs=pl.BlockSpec((1, t, LANES), lambda b, j: (b, 0, j)),
        out_shape=jax.ShapeDtypeStruct((nb, t, SSD_XBC), F32),
        compiler_params=_params(("parallel", "parallel")),
        name="ssd_conv",
    )(proj, conv_w, conv_b.reshape(1, -1))


def _ssd_kernel(*refs, rev):
    if rev:
        (x_ref, bc_ref, gc_ref, gr_ref, dbr_ref, dbc_ref, alr_ref, alc_ref, prev_ref, z_ref, skip_ref, gain_ref,
         o_ref, s_ref, xw_ref, y_ref) = refs
    else:
        x_ref, bc_ref, gc_ref, gr_ref, dbr_ref, dbc_ref, alr_ref, alc_ref, o_ref, s_ref, xw_ref = refs
    L = SSD_CHUNK
    hpg = SSD_HEADS // SSD_GROUPS
    gw = hpg * SSD_HEAD_DIM

    d0 = SSD_HEADS if rev else 0
    dt_col = _softplus(gc_ref[0][:, d0:d0 + SSD_HEADS] + dbr_ref[:, d0:d0 + SSD_HEADS])
    dt_row = _softplus(gr_ref[0, 0][d0:d0 + SSD_HEADS, :] + dbc_ref[d0:d0 + SSD_HEADS, :])
    a_col = dt_col * -jnp.exp(alr_ref[:, d0:d0 + SSD_HEADS])
    a_row = dt_row * -jnp.exp(alc_ref[d0:d0 + SSD_HEADS, :])
    mask = _time_tri(L, rev)
    tri = mask.astype(F32)
    ac_col = _dot_f32(tri, a_col)
    ac_row = _dot_nt_f32(a_row, tri)
    last = 0 if rev else L - 1
    x = x_ref[0]
    bc = bc_ref[0]

    for g in range(SSD_GROUPS):
        bm = bc[:, g * SSD_STATE:(g + 1) * SSD_STATE]
        cm = bc[:, SSD_BC_W + g * SSD_STATE:SSD_BC_W + (g + 1) * SSD_STATE]
        cb = _dot_nt(cm, bm)
        state = s_ref[g]
        inter = _dot_nt(cm, state)
        for hh in range(hpg):
            h = g * hpg + hh
            cols = slice(h * SSD_HEAD_DIM, (h + 1) * SSD_HEAD_DIM)
            gcols = slice(hh * SSD_HEAD_DIM, (hh + 1) * SSD_HEAD_DIM)
            acol, arow = ac_col[:, h:h + 1], ac_row[h:h + 1, :]
            decay = jnp.exp(jnp.where(mask, acol - arow, NEG))
            xh = x[:, cols]
            y = _dot(cb * decay * dt_row[h:h + 1, :], xh) + jnp.exp(acol) * inter[:, gcols]
            a_last = acol[last:last + 1, :]
            xw_ref[:, gcols] = xh * (jnp.exp(a_last - acol) * dt_col[:, h:h + 1])
            s_ref[g, gcols, :] = state[gcols, :] * jnp.exp(a_last)
            if rev:
                y_ref[:, cols] = y + prev_ref[0][:, cols]
            else:
                o_ref[0, :, cols] = y
        s_ref[g] += _dot_tn(xw_ref[...], bm)

    if rev:
        yt = (y_ref[...] + skip_ref[...] * x) * _silu(z_ref[0])
        for g in range(SSD_GROUPS):
            cols = slice(g * gw, (g + 1) * gw)
            o_ref[0, :, cols] = (_rms(yt[:, cols]) * gain_ref[:, cols]).astype(o_ref.dtype)


def ssd(proj, gates, xbc, dt_bias, a_log, d_skip, ssd_gain, t_lat):
    nb, t, _ = proj.shape
    L = SSD_CHUNK
    n_lat, n_ctx = t_lat // L, (t - t_lat) // L
    gate_rows = _gate_rows(gates, L)
    skip = jnp.repeat(d_skip.astype(F32), SSD_HEAD_DIM).reshape(1, SSD_W)

    def call(rev, prev):
        idx = lambda b, s: _chunk_index(s, n_lat, n_ctx, rev)
        ns = SSD_SAMPLES_PER_STEP
        tok = lambda width, col: pl.BlockSpec((ns, L, width), lambda b, s: (b, idx(b, s), col // width))
        const = lambda shape: pl.BlockSpec(shape, lambda b, s: (0, 0))
        in_specs = [tok(SSD_W, 0), tok(2 * SSD_BC_W, SSD_W),
                    pl.BlockSpec((ns, L, LANES), lambda b, s: (b, idx(b, s), 0)),
                    pl.BlockSpec((ns, 1, GATE_ROWS, L), lambda b, s: (b, idx(b, s), 0, 0)),
                    const((1, LANES)), const((GATE_ROWS, 1)), const((1, LANES)), const((GATE_ROWS, 1))]
        args = [xbc, xbc, gates, gate_rows, _lane_row(dt_bias), _sublane_col(dt_bias),
                _lane_row(a_log), _sublane_col(a_log)]
        per_sample = [True] * 4 + [False] * 4
        scratch = [_sample_scratch(ns, (SSD_GROUPS, SSD_W // SSD_GROUPS, SSD_STATE)),
                   _sample_scratch(ns, (L, SSD_W // SSD_GROUPS))]
        if rev:
            in_specs += [tok(SSD_W, 0), tok(SSD_W, ODD_Z), const((1, SSD_W)), const((1, SSD_W))]
            args += [prev, proj, skip, ssd_gain.reshape(1, -1)]
            per_sample += [True, True, False, False]
            scratch += [_sample_scratch(ns, (L, SSD_W))]
        return pl.pallas_call(
            _for_each_sample(functools.partial(_ssd_kernel, rev=rev), per_sample + [True], len(scratch),
                             step_axis=1),
            grid=(nb // ns, n_lat + n_ctx),
            in_specs=in_specs,
            out_specs=tok(SSD_W, 0),
            out_shape=jax.ShapeDtypeStruct((nb, t, SSD_W), BF16 if rev else F32),
            scratch_shapes=scratch,
            compiler_params=_params(("parallel", "arbitrary")),
            name="ssd_bwd" if rev else "ssd_fwd",
        )(*args)

    return call(True, call(False, None))


def _split_w_in(w_in, n_gate):
    n_main = w_in.shape[1] - n_gate
    w_gate = jnp.zeros((w_in.shape[0], LANES), F32).at[:, :n_gate].set(w_in[:, n_main:])
    return w_in[:, :n_main].astype(BF16), w_gate


def kernel(x, c, ctx, c_ctx, hgrn_lb_logits,
           l0_mod_w, l0_mod_b, l0_w_in, l0_q_gain, l0_k_gain, l0_rpb, l0_gate_bias, l0_ml_gain,
           l0_w_out, l0_ffn_w1, l0_ffn_w3, l0_ffn_w2,
           l1_mod_w, l1_mod_b, l1_w_in, l1_hg_gain, l1_conv_w, l1_conv_b, l1_dt_bias, l1_a_log,
           l1_d_skip, l1_ssd_gain, l1_w_out, l1_ffn_w1, l1_ffn_w3, l1_ffn_w2):
    nb, t_lat, d = x.shape
    p = jax.nn.softmax(hgrn_lb_logits.astype(F32), axis=0)
    lower_bounds = jnp.cumsum(p, axis=0) - p[0]

    cvecs = jnp.zeros((8, d), F32).at[:nb].set(c).at[nb].set(c_ctx)
    xs = jnp.concatenate([x, ctx], axis=1)

    layers = [
        (l0_mod_w, l0_mod_b, l0_w_in, 4 * ML_HEADS, l0_w_out, l0_ffn_w1, l0_ffn_w3, l0_ffn_w2),
        (l1_mod_w, l1_mod_b, l1_w_in, 2 * SSD_HEADS, l1_w_out, l1_ffn_w1, l1_ffn_w3, l1_ffn_w2),
    ]
    for layer, (mod_w, mod_b, w_in, n_gate, w_out, w1, w3, w2) in enumerate(layers):
        mods = modulation_vectors(cvecs, mod_w, mod_b).reshape(8, N_MOD, 1, d)
        sh1, sc1, g1, sh2, sc2, g2 = [mods[:nb + 1, i] for i in range(N_MOD)]
        w_main, w_gate = _split_w_in(w_in, n_gate)
        proj, gates = modulated_in_proj(xs, sh1, sc1, w_main, w_gate, t_lat)
        if layer == 0:
            bias = na_bias_tables(l0_rpb, t_lat // GRID_W)
            ya = neighbourhood_attention(proj, bias, l0_q_gain, l0_k_gain, t_lat)
            yb = mlstm(proj, gates, l0_gate_bias, l0_ml_gain, t_lat)
        else:
            ya = hgrn2(proj, lower_bounds[layer], l1_hg_gain, t_lat)
            xbc = ssd_conv(proj, l1_conv_w, l1_conv_b, t_lat)
            yb = ssd(proj, gates, xbc, l1_dt_bias, l1_a_log, l1_d_skip, l1_ssd_gain, t_lat)
        ka = ya.shape[-1]
        t_out = xs.shape[1] if layer == 0 else t_lat
        xs = out_proj_residual(ya, yb, xs, g1, w_out[:ka].astype(BF16), w_out[ka:].astype(BF16), t_lat, t_out)
        xs = modulated_ffn_residual(xs, sh2, sc2, g2, w1.astype(BF16), w3.astype(BF16), w2.astype(BF16), t_lat)
    return xs
```

```python
import functools

import jax
import jax.numpy as jnp
import numpy as np
from jax import lax
from jax.experimental import pallas as pl
from jax.experimental.pallas import tpu as pltpu

EPS = 1e-6
N_MOD = 6
GRID_W = 64

NA_HEADS = 8
NA_HEAD_DIM = 128
NA_WIN_H = 8
NA_WIN_W = 16
NA_GROUP_ROWS = 4
NA_BAND_ROWS = NA_GROUP_ROWS + NA_WIN_H - 1
NA_GROUP = NA_GROUP_ROWS * GRID_W
NA_BAND = NA_BAND_ROWS * GRID_W

ML_HEADS = 4
ML_QK_DIM = 128
ML_V_DIM = 256
ML_CHUNK = 64

HG_HEADS = 8
HG_K_DIM = 128
HG_V_DIM = 128
HG_CHUNK = 32
HG_SUB = 8

SSD_HEADS = 16
SSD_HEAD_DIM = 64
SSD_GROUPS = 2
SSD_STATE = 128
SSD_CONV = 5
SSD_CHUNK = 64

NA_W = NA_HEADS * NA_HEAD_DIM
ML_QK_W = ML_HEADS * ML_QK_DIM
ML_W = ML_HEADS * ML_V_DIM
HG_K_W = HG_HEADS * HG_K_DIM
HG_W = HG_HEADS * HG_V_DIM
SSD_W = SSD_HEADS * SSD_HEAD_DIM
SSD_BC_W = SSD_GROUPS * SSD_STATE
SSD_XBC = SSD_W + 2 * SSD_BC_W

EVEN_Q, EVEN_K, EVEN_V = 0, NA_W, 2 * NA_W
EVEN_MQ = 3 * NA_W
EVEN_MK = EVEN_MQ + ML_QK_W
EVEN_MV = EVEN_MK + ML_QK_W
EVEN_MO = EVEN_MV + ML_W
ODD_Q, ODD_FF, ODD_FB = 0, HG_K_W, 2 * HG_K_W
ODD_I = 3 * HG_K_W
ODD_G = ODD_I + HG_W
ODD_Z = ODD_G + HG_W
ODD_XBC = ODD_Z + SSD_W

VMEM_LIMIT_BYTES = 56 * 1024 * 1024
LANES = 128
GATE_ROWS = 32
NEG = -1e30
ML_SAMPLES_PER_STEP = 4
HG_SAMPLES_PER_STEP = 4
SSD_SAMPLES_PER_STEP = 1

F32 = jnp.float32
BF16 = jnp.bfloat16
HIGHEST = lax.Precision.HIGHEST


def _params(sem):
    return pltpu.CompilerParams(dimension_semantics=sem, vmem_limit_bytes=VMEM_LIMIT_BYTES)


def _dot(a, b):
    return jnp.dot(a.astype(BF16), b.astype(BF16), preferred_element_type=F32)


def _dot_nt(a, b):
    return lax.dot_general(a.astype(BF16), b.astype(BF16), (((1,), (1,)), ((), ())), preferred_element_type=F32)


def _dot_tn(a, b):
    return lax.dot_general(a.astype(BF16), b.astype(BF16), (((0,), (0,)), ((), ())), preferred_element_type=F32)


def _dot_f32(a, b):
    return jnp.dot(a, b, precision=HIGHEST, preferred_element_type=F32)


def _dot_nt_f32(a, b):
    return lax.dot_general(a, b, (((1,), (1,)), ((), ())), precision=HIGHEST, preferred_element_type=F32)


def _sigmoid(x):
    return 1.0 / (1.0 + jnp.exp(-x))


def _silu(x):
    return x * _sigmoid(x)


def _log_sigmoid(x):
    return jnp.minimum(x, 0.0) - jnp.log(1.0 + jnp.exp(-jnp.abs(x)))


def _softplus(x):
    return jnp.maximum(x, 0.0) + jnp.log(1.0 + jnp.exp(-jnp.abs(x)))


def _rms(x):
    return x * lax.rsqrt(jnp.mean(x * x, axis=-1, keepdims=True) + EPS)


def _time_tri(n, rev):
    t = lax.broadcasted_iota(jnp.int32, (n, n), 0)
    u = lax.broadcasted_iota(jnp.int32, (n, n), 1)
    return (u >= t) if rev else (u <= t)


def _chunk_index(step, n_lat, n_ctx, rev):
    if rev:
        return n_lat + n_ctx - 1 - step
    return jnp.where(step < n_ctx, n_lat + step, step - n_ctx)


def _mod_kernel(c_ref, w_ref, b_ref, o_ref):
    o_ref[...] = jnp.dot(_silu(c_ref[...]), w_ref[...], preferred_element_type=F32) + b_ref[...]


def modulation_vectors(cvecs, mod_w, mod_b, tn=1024):
    r, d = cvecs.shape
    n = mod_w.shape[1]
    return pl.pallas_call(
        _mod_kernel,
        grid=(n // tn,),
        in_specs=[pl.BlockSpec((r, d), lambda j: (0, 0)),
                  pl.BlockSpec((d, tn), lambda j: (0, j)),
                  pl.BlockSpec((1, tn), lambda j: (0, j))],
        out_specs=pl.BlockSpec((r, tn), lambda j: (0, j)),
        out_shape=jax.ShapeDtypeStruct((r, n), F32),
        compiler_params=_params(("arbitrary",)),
        name="modulation",
    )(cvecs, mod_w, mod_b.reshape(1, n))


def _mod_map(n_lat_blocks, n_batch):
    return lambda b, i, j: (jnp.where(i >= n_lat_blocks, n_batch, b), 0, 0)


def _inproj_kernel(x_ref, sh_ref, sc_ref, w_ref, wg_ref, o_ref, og_ref, h_ref):
    @pl.when(pl.program_id(2) == 0)
    def _():
        h = _rms(x_ref[0]) * (1.0 + sc_ref[0]) + sh_ref[0]
        h_ref[...] = h.astype(BF16)
        og_ref[0] = jnp.dot(h_ref[...], wg_ref[...], preferred_element_type=F32)

    o_ref[0] = jnp.dot(h_ref[...], w_ref[...], preferred_element_type=F32)


def modulated_in_proj(x, shift, scale, w_main, w_gate, t_lat, tm=512, tn=1536):
    nb, t, d = x.shape
    n = w_main.shape[1]
    mod_map = _mod_map(t_lat // tm, nb)
    return pl.pallas_call(
        _inproj_kernel,
        grid=(nb, pl.cdiv(t, tm), n // tn),
        in_specs=[pl.BlockSpec((1, tm, d), lambda b, i, j: (b, i, 0)),
                  pl.BlockSpec((1, 1, d), mod_map),
                  pl.BlockSpec((1, 1, d), mod_map),
                  pl.BlockSpec((d, tn), lambda b, i, j: (0, j)),
                  pl.BlockSpec((d, LANES), lambda b, i, j: (0, 0))],
        out_specs=[pl.BlockSpec((1, tm, tn), lambda b, i, j: (b, i, j)),
                   pl.BlockSpec((1, tm, LANES), lambda b, i, j: (b, i, 0))],
        out_shape=[jax.ShapeDtypeStruct((nb, t, n), F32),
                   jax.ShapeDtypeStruct((nb, t, LANES), F32)],
        scratch_shapes=[pltpu.VMEM((tm, d), BF16)],
        compiler_params=_params(("parallel", "parallel", "arbitrary")),
        name="in_proj",
    )(x, shift, scale, w_main, w_gate)


def _outproj_kernel(ya_ref, yb_ref, x_ref, g_ref, wa_ref, wb_ref, o_ref):
    acc = jnp.dot(ya_ref[0], wa_ref[...], preferred_element_type=F32)
    acc += jnp.dot(yb_ref[0], wb_ref[...], preferred_element_type=F32)
    o_ref[0] = x_ref[0] + g_ref[0] * acc


def out_proj_residual(ya, yb, x, gate, w_a, w_b, t_lat, t_out, tm=512, tn=2048):
    nb, _, k = ya.shape
    n = w_a.shape[1]
    mod_map = _mod_map(t_lat // tm, nb)
    return pl.pallas_call(
        _outproj_kernel,
        grid=(nb, pl.cdiv(t_out, tm), n // tn),
        in_specs=[pl.BlockSpec((1, tm, k), lambda b, i, j: (b, i, 0)),
                  pl.BlockSpec((1, tm, k), lambda b, i, j: (b, i, 0)),
                  pl.BlockSpec((1, tm, tn), lambda b, i, j: (b, i, j)),
                  pl.BlockSpec((1, 1, tn), lambda b, i, j: mod_map(b, i, j)[:2] + (j,)),
                  pl.BlockSpec((k, tn), lambda b, i, j: (0, j)),
                  pl.BlockSpec((k, tn), lambda b, i, j: (0, j))],
        out_specs=pl.BlockSpec((1, tm, tn), lambda b, i, j: (b, i, j)),
        out_shape=jax.ShapeDtypeStruct((nb, t_out, n), F32),
        compiler_params=_params(("parallel", "parallel", "arbitrary")),
        name="out_proj",
    )(ya, yb, x, gate, w_a, w_b)


def _ffn_kernel(x_ref, sh_ref, sc_ref, g_ref, w1_ref, w3_ref, w2_ref, o_ref, h_ref, acc_ref):
    j = pl.program_id(2)

    @pl.when(j == 0)
    def _():
        h_ref[...] = (_rms(x_ref[0]) * (1.0 + sc_ref[0]) + sh_ref[0]).astype(BF16)
        acc_ref[...] = jnp.zeros_like(acc_ref)

    h = h_ref[...]
    a = jnp.dot(h, w1_ref[...], preferred_element_type=F32)
    b = jnp.dot(h, w3_ref[...], preferred_element_type=F32)
    acc_ref[...] += jnp.dot((_silu(a) * b).astype(BF16), w2_ref[...], preferred_element_type=F32)

    @pl.when(j == pl.num_programs(2) - 1)
    def _():
        o_ref[0] = x_ref[0] + g_ref[0] * acc_ref[...]


def modulated_ffn_residual(x, shift, scale, gate, w1, w3, w2, t_lat, tm=512, th=512):
    nb, t, d = x.shape
    hid = w1.shape[1]
    mod_map = _mod_map(t_lat // tm, nb)
    return pl.pallas_call(
        _ffn_kernel,
        grid=(nb, pl.cdiv(t, tm), hid // th),
        in_specs=[pl.BlockSpec((1, tm, d), lambda b, i, j: (b, i, 0)),
                  pl.BlockSpec((1, 1, d), mod_map),
                  pl.BlockSpec((1, 1, d), mod_map),
                  pl.BlockSpec((1, 1, d), mod_map),
                  pl.BlockSpec((d, th), lambda b, i, j: (0, j)),
                  pl.BlockSpec((d, th), lambda b, i, j: (0, j)),
                  pl.BlockSpec((th, d), lambda b, i, j: (j, 0))],
        out_specs=pl.BlockSpec((1, tm, d), lambda b, i, j: (b, i, 0)),
        out_shape=jax.ShapeDtypeStruct((nb, t, d), F32),
        scratch_shapes=[pltpu.VMEM((tm, d), BF16), pltpu.VMEM((tm, d), F32)],
        compiler_params=_params(("parallel", "parallel", "arbitrary")),
        name="ffn",
    )(x, shift, scale, gate, w1, w3, w2)


def _na_geometry(rows):
    n_groups = rows // NA_GROUP_ROWS
    band0 = np.clip(np.arange(n_groups) * NA_GROUP_ROWS - NA_WIN_H // 2, 0, rows - NA_BAND_ROWS)
    return n_groups, band0


def na_bias_tables(rpb, rows):
    n_groups, band0 = _na_geometry(rows)
    tables = []
    for g in (0, 1, n_groups - 1):
        qr = g * NA_GROUP_ROWS + np.arange(NA_GROUP_ROWS)
        r0 = np.clip(qr - NA_WIN_H // 2, 0, rows - NA_WIN_H)
        kr = band0[g] + np.arange(NA_BAND_ROWS)
        qc = np.arange(GRID_W)
        kc = np.arange(GRID_W)
        w0 = np.clip(qc - NA_WIN_W // 2, 0, GRID_W - NA_WIN_W)
        row_ok = (kr[None, :] >= r0[:, None]) & (kr[None, :] < r0[:, None] + NA_WIN_H)
        col_ok = (kc[None, :] >= w0[:, None]) & (kc[None, :] < w0[:, None] + NA_WIN_W)
        drow = np.clip(kr[None, :] - qr[:, None] + NA_WIN_H - 1, 0, 2 * NA_WIN_H - 2)
        dcol = np.clip(kc[None, :] - qc[:, None], -(NA_WIN_W - 1), NA_WIN_W - 1) + NA_WIN_W - 1
        ok = row_ok[:, None, :, None] & col_ok[None, :, None, :]
        row_sel = np.eye(2 * NA_WIN_H - 1, dtype=np.float32)[drow]
        col_sel = np.eye(2 * NA_WIN_W - 1, dtype=np.float32)[dcol]
        by_row = jnp.einsum('hab,rka->hrkb', rpb.astype(F32), row_sel, precision=HIGHEST)
        bias = jnp.einsum('hrkb,cdb->hrckd', by_row, col_sel, precision=HIGHEST)
        tables.append(jnp.where(jnp.asarray(ok), bias, NEG).reshape(NA_HEADS, NA_GROUP, NA_BAND))
    return jnp.stack(tables)


def _na_kernel(q_ref, k_ref, v_ref, bias_ref, qg_ref, kg_ref, o_ref, *, rows, t_lat):
    g = pl.program_id(2)
    n_groups = rows // NA_GROUP_ROWS
    qn = (_rms(q_ref[0]) * qg_ref[...] * NA_HEAD_DIM ** -0.5).astype(BF16)
    kc = (_rms(k_ref[0, t_lat:, :]) * kg_ref[...]).astype(BF16)
    vc = v_ref[0, t_lat:, :].astype(BF16)
    s_ctx = _dot_nt(qn, kc)

    @pl.when(g < n_groups)
    def _():
        band0 = jnp.clip(g * NA_GROUP_ROWS - NA_WIN_H // 2, 0, rows - NA_BAND_ROWS)
        start = pl.multiple_of(band0 * GRID_W, GRID_W)
        kb = (_rms(k_ref[0, pl.ds(start, NA_BAND), :]) * kg_ref[...]).astype(BF16)
        vb = v_ref[0, pl.ds(start, NA_BAND), :].astype(BF16)
        s_loc = _dot_nt(qn, kb) + bias_ref[0, 0]
        m = jnp.maximum(jnp.max(s_loc, axis=-1, keepdims=True), jnp.max(s_ctx, axis=-1, keepdims=True))
        p_loc = jnp.exp(s_loc - m)
        p_ctx = jnp.exp(s_ctx - m)
        den = jnp.sum(p_loc, axis=-1, keepdims=True) + jnp.sum(p_ctx, axis=-1, keepdims=True)
        o_ref[0] = ((_dot(p_loc, vb) + _dot(p_ctx, vc)) / den).astype(o_ref.dtype)

    @pl.when(g >= n_groups)
    def _():
        p = jnp.exp(s_ctx - jnp.max(s_ctx, axis=-1, keepdims=True))
        o_ref[0] = (_dot(p, vc) / jnp.sum(p, axis=-1, keepdims=True)).astype(o_ref.dtype)


def neighbourhood_attention(proj, bias, q_gain, k_gain, t_lat):
    nb, t, _ = proj.shape
    rows = t_lat // GRID_W
    n_groups, _ = _na_geometry(rows)
    assert t - t_lat == NA_GROUP
    table = lambda b, h, g: (jnp.where(g == 0, 0, jnp.where(g >= n_groups - 1, 2, 1)), h, 0, 0)
    return pl.pallas_call(
        functools.partial(_na_kernel, rows=rows, t_lat=t_lat),
        grid=(nb, NA_HEADS, n_groups + 1),
        in_specs=[pl.BlockSpec((1, NA_GROUP, NA_HEAD_DIM), lambda b, h, g: (b, g, EVEN_Q // NA_HEAD_DIM + h)),
                  pl.BlockSpec((1, t, NA_HEAD_DIM), lambda b, h, g: (b, 0, EVEN_K // NA_HEAD_DIM + h)),
                  pl.BlockSpec((1, t, NA_HEAD_DIM), lambda b, h, g: (b, 0, EVEN_V // NA_HEAD_DIM + h)),
                  pl.BlockSpec((1, 1, NA_GROUP, NA_BAND), table),
                  pl.BlockSpec((1, NA_HEAD_DIM), lambda b, h, g: (0, 0)),
                  pl.BlockSpec((1, NA_HEAD_DIM), lambda b, h, g: (0, 0))],
        out_specs=pl.BlockSpec((1, NA_GROUP, NA_HEAD_DIM), lambda b, h, g: (b, g, h)),
        out_shape=jax.ShapeDtypeStruct((nb, t, NA_W), BF16),
        compiler_params=_params(("parallel", "parallel", "arbitrary")),
        name="natten",
    )(proj, proj, proj, bias, q_gain.reshape(1, -1), k_gain.reshape(1, -1))


def _mlstm_kernel(*refs, rev):
    if rev:
        (q_ref, k_ref, v_ref, gc_ref, gr_ref, gbr_ref, gbc_ref, prev_ref, og_ref, gain_ref,
         o_ref, c_ref, n_ref, m_ref) = refs
    else:
        q_ref, k_ref, v_ref, gc_ref, gr_ref, gbr_ref, gbc_ref, o_ref, c_ref, n_ref, m_ref = refs
    L = ML_CHUNK

    base = 2 * ML_HEADS if rev else 0
    gcol = gc_ref[0][:, :4 * ML_HEADS] + gbr_ref[:, :4 * ML_HEADS]
    grow = gr_ref[0, 0][:4 * ML_HEADS, :] + gbc_ref[:4 * ML_HEADS, :]
    li_col = gcol[:, base:base + ML_HEADS]
    lf_col = _log_sigmoid(gcol[:, base + ML_HEADS:base + 2 * ML_HEADS])
    li_row = grow[base:base + ML_HEADS, :]
    lf_row = _log_sigmoid(grow[base + ML_HEADS:base + 2 * ML_HEADS, :])
    mask = _time_tri(L, rev)
    tri = mask.astype(F32)
    bc_col = _dot_f32(tri, lf_col)
    bc_row = _dot_nt_f32(lf_row, tri)
    last = 0 if rev else L - 1

    for h in range(ML_HEADS):
        bcol, brow = bc_col[:, h:h + 1], bc_row[h:h + 1, :]
        licol, lirow = li_col[:, h:h + 1], li_row[h:h + 1, :]
        m_prev = m_ref[h]
        q = q_ref[0][:, h * ML_QK_DIM:(h + 1) * ML_QK_DIM] * ML_QK_DIM ** -0.5
        k = k_ref[0][:, h * ML_QK_DIM:(h + 1) * ML_QK_DIM]
        v = v_ref[0][:, h * ML_V_DIM:(h + 1) * ML_V_DIM]
        dlog = jnp.where(mask, bcol - brow + lirow, NEG)
        inter = bcol + m_prev
        m_t = jnp.maximum(inter, jnp.max(dlog, axis=-1, keepdims=True))
        w_inter = jnp.exp(inter - m_t)
        s = _dot_nt(q, k) * jnp.exp(dlog - m_t)
        c_state, n_state = c_ref[h], n_ref[h]
        num = w_inter * _dot_nt(q, c_state) + _dot(s, v)
        den = w_inter * jnp.sum(q * n_state, axis=-1, keepdims=True) + jnp.sum(s, axis=-1, keepdims=True)
        hid = num / jnp.maximum(jnp.abs(den), jnp.exp(-m_t))
        b_last = bcol[last:last + 1, :]
        gg = b_last - bcol + licol
        m_new = jnp.maximum(b_last + m_prev, jnp.max(gg, axis=0, keepdims=True))
        decay = jnp.exp(b_last + m_prev - m_new)
        w = jnp.exp(gg - m_new)
        c_ref[h] = decay * c_state + _dot_tn(v * w, k)
        n_ref[h] = decay * n_state + jnp.sum(k * w, axis=0, keepdims=True)
        m_ref[h] = m_new
        cols = slice(h * ML_V_DIM, (h + 1) * ML_V_DIM)
        if rev:
            tot = hid + prev_ref[0][:, cols]
            y = _rms(tot) * gain_ref[:, cols] * _sigmoid(og_ref[0][:, cols])
            o_ref[0, :, cols] = y.astype(o_ref.dtype)
        else:
            o_ref[0, :, cols] = hid


def _for_each_sample(body, per_sample, n_scratch, step_axis):
    def kern(*refs):
        n_io = len(refs) - n_scratch

        @pl.when(pl.program_id(step_axis) == 0)
        def _():
            for r in refs[n_io:]:
                r[...] = jnp.zeros_like(r)

        for bb in range(refs[-1].shape[0]):
            views = [r.at[bb:bb + 1] if per_sample[i] else r for i, r in enumerate(refs[:n_io])]
            body(*views, *[r.at[bb] for r in refs[n_io:]])
    return kern


def _sample_scratch(n_samples, shape):
    return pltpu.VMEM((n_samples,) + shape, F32)


def _gate_rows(gates, chunk):
    nb, t, _ = gates.shape
    return gates[..., :GATE_ROWS].reshape(nb, t // chunk, chunk, GATE_ROWS).transpose(0, 1, 3, 2)


def _lane_row(v):
    return jnp.zeros((1, LANES), F32).at[0, :v.size].set(v.reshape(-1).astype(F32))


def _sublane_col(v):
    return jnp.zeros((GATE_ROWS, 1), F32).at[:v.size, 0].set(v.reshape(-1).astype(F32))


def mlstm(proj, gates, gate_bias, ml_gain, t_lat):
    nb, t, _ = proj.shape
    L = ML_CHUNK
    n_lat, n_ctx = t_lat // L, (t - t_lat) // L
    gate_rows = _gate_rows(gates, L)
    gb_row, gb_col = _lane_row(gate_bias), _sublane_col(gate_bias)

    def call(rev, prev):
        idx = lambda b, s: _chunk_index(s, n_lat, n_ctx, rev)
        ns = ML_SAMPLES_PER_STEP
        tok = lambda width, col: pl.BlockSpec((ns, L, width), lambda b, s: (b, idx(b, s), col // width))
        in_specs = [tok(ML_QK_W, EVEN_MQ), tok(ML_QK_W, EVEN_MK), tok(ML_W, EVEN_MV),
                    pl.BlockSpec((ns, L, LANES), lambda b, s: (b, idx(b, s), 0)),
                    pl.BlockSpec((ns, 1, GATE_ROWS, L), lambda b, s: (b, idx(b, s), 0, 0)),
                    pl.BlockSpec((1, LANES), lambda b, s: (0, 0)),
                    pl.BlockSpec((GATE_ROWS, 1), lambda b, s: (0, 0))]
        args = [proj, proj, proj, gates, gate_rows, gb_row, gb_col]
        per_sample = [True] * 5 + [False] * 2
        if rev:
            in_specs += [tok(ML_W, 0), tok(ML_W, EVEN_MO), pl.BlockSpec((1, ML_W), lambda b, s: (0, 0))]
            args += [prev, proj, ml_gain.reshape(1, -1)]
            per_sample += [True, True, False]
        return pl.pallas_call(
            _for_each_sample(functools.partial(_mlstm_kernel, rev=rev), per_sample + [True], 3, step_axis=1),
            grid=(nb // ns, n_lat + n_ctx),
            in_specs=in_specs,
            out_specs=tok(ML_W, 0),
            out_shape=jax.ShapeDtypeStruct((nb, t, ML_W), BF16 if rev else F32),
            scratch_shapes=[_sample_scratch(ns, (ML_HEADS, ML_V_DIM, ML_QK_DIM)),
                            _sample_scratch(ns, (ML_HEADS, 1, ML_QK_DIM)),
                            _sample_scratch(ns, (ML_HEADS, 1, 1))],
            compiler_params=_params(("parallel", "arbitrary")),
            name="mlstm_bwd" if rev else "mlstm_fwd",
        )(*args)

    return call(True, call(False, None))


def _hgrn_kernel(*refs, rev, heads_per_step):
    if rev:
        q_ref, f_ref, i_ref, lb_ref, prev_ref, og_ref, gain_ref, o_ref, s_ref, p_ref, od_ref = refs
    else:
        q_ref, f_ref, i_ref, lb_ref, o_ref, s_ref, p_ref, od_ref = refs
    L, SUB = HG_CHUNK, HG_SUB
    half = L // 2

    mask = _time_tri(L, rev)
    tri = mask.astype(F32)
    t_id = lax.broadcasted_iota(jnp.int32, (L, L), 0)
    u_id = lax.broadcasted_iota(jnp.int32, (L, L), 1)
    row_id = lax.broadcasted_iota(jnp.int32, (L, 1), 0)
    sub_id = lax.broadcasted_iota(jnp.int32, (SUB, 1), 0)
    same_half = (t_id // half) == (u_id // half)
    if rev:
        m1 = (t_id < half) & (u_id >= half)
        m2 = same_half & (t_id % half < SUB) & (u_id % half >= SUB)
        r1, r2a, r2b, last = half, SUB, half + SUB, 0
    else:
        m1 = (t_id >= half) & (u_id < half)
        m2 = same_half & (t_id % half >= SUB) & (u_id % half < SUB)
        r1, r2a, r2b, last = half - 1, SUB - 1, half + SUB - 1, L - 1
    ones = jnp.ones((HG_K_DIM, LANES), BF16)

    for h in range(heads_per_step):
        cols = slice(h * HG_K_DIM, (h + 1) * HG_K_DIM)
        lb = lb_ref[:, cols]
        q = _silu(q_ref[0][:, cols])
        v = _silu(i_ref[0][:, cols])
        f = lb + (1.0 - lb) * _sigmoid(f_ref[0][:, cols])
        kk = 1.0 - f
        G = _dot_f32(tri, jnp.log(f))
        a1 = G[r1:r1 + 1, :]
        a2 = jnp.where(row_id < half, G[r2a:r2a + 1, :], G[r2b:r2b + 1, :])
        A1 = _dot_nt(q * jnp.exp(jnp.minimum(G - a1, 0.0)), kk * jnp.exp(jnp.minimum(a1 - G, 0.0)))
        A2 = _dot_nt(q * jnp.exp(jnp.minimum(G - a2, 0.0)), kk * jnp.exp(jnp.minimum(a2 - G, 0.0)))
        a_off = jnp.where(m1, A1, jnp.where(m2, A2, 0.0))
        for blk in range(L // SUB):
            kb = kk[blk * SUB:(blk + 1) * SUB, :]
            gb = G[blk * SUB:(blk + 1) * SUB, :]
            for t in range(SUB):
                r = blk * SUB + t
                ok = (sub_id >= t) if rev else (sub_id <= t)
                dec = jnp.exp(jnp.where(ok, G[r:r + 1, :] - gb, NEG))
                p_ref[h, r * SUB:(r + 1) * SUB, :] = q[r:r + 1, :] * kb * dec
        rsum = _dot(p_ref[h], ones)
        for blk in range(L // SUB):
            vb = v[blk * SUB:(blk + 1) * SUB, :]
            for t in range(SUB):
                r = blk * SUB + t
                od_ref[h, r:r + 1, :] = jnp.sum(rsum[r * SUB:(r + 1) * SUB, :] * vb, axis=0, keepdims=True)
        st = s_ref[h]
        o = _dot(a_off, v) + od_ref[h] + _dot_nt(q * jnp.exp(G), st)
        gl = G[last:last + 1, :]
        s_ref[h] = st * jnp.exp(gl) + _dot_tn(v, kk * jnp.exp(gl - G))
        if rev:
            tot = o + prev_ref[0][:, cols]
            y = _rms(tot) * gain_ref[:, cols] * _silu(og_ref[0][:, cols])
            o_ref[0, :, cols] = y.astype(o_ref.dtype)
        else:
            o_ref[0, :, cols] = o


def hgrn2(proj, lower_bound, hg_gain, t_lat, heads_per_step=HG_HEADS):
    nb, t, _ = proj.shape
    L = HG_CHUNK
    n_lat, n_ctx = t_lat // L, (t - t_lat) // L
    width = heads_per_step * HG_K_DIM
    n_hb = HG_HEADS // heads_per_step

    def call(rev, prev):
        idx = lambda s: _chunk_index(s, n_lat, n_ctx, rev)
        ns = HG_SAMPLES_PER_STEP
        tok = lambda col: pl.BlockSpec((ns, L, width), lambda b, hb, s: (b, idx(s), col // width + hb))
        vec = pl.BlockSpec((1, width), lambda b, hb, s: (0, hb))
        in_specs = [tok(ODD_Q), tok(ODD_FB if rev else ODD_FF), tok(ODD_I), vec]
        args = [proj, proj, proj, lower_bound.reshape(1, -1)]
        per_sample = [True] * 3 + [False]
        if rev:
            in_specs += [tok(0), tok(ODD_G), vec]
            args += [prev, proj, hg_gain.reshape(1, -1)]
            per_sample += [True, True, False]
        return pl.pallas_call(
            _for_each_sample(functools.partial(_hgrn_kernel, rev=rev, heads_per_step=heads_per_step),
                             per_sample + [True], 3, step_axis=2),
            grid=(nb // ns, n_hb, n_lat + n_ctx),
            in_specs=in_specs,
            out_specs=tok(0),
            out_shape=jax.ShapeDtypeStruct((nb, t, HG_W), BF16 if rev else F32),
            scratch_shapes=[_sample_scratch(ns, (heads_per_step, HG_V_DIM, HG_K_DIM)),
                            _sample_scratch(ns, (heads_per_step, L * HG_SUB, HG_K_DIM)),
                            _sample_scratch(ns, (heads_per_step, L, HG_V_DIM))],
            compiler_params=_params(("parallel", "parallel", "arbitrary")),
            name="hgrn_bwd" if rev else "hgrn_fwd",
        )(*args)

    return call(True, call(False, None))


def _conv_kernel(x_ref, w_ref, b_ref, o_ref, *, t_lat):
    x = x_ref[0]
    n = x.shape[0]
    row = lax.broadcasted_iota(jnp.int32, (n, 1), 0)
    acc = jnp.zeros_like(x) + b_ref[...]
    for j in range(SSD_CONV):
        d = j - SSD_CONV // 2
        src = row + d
        ok = (src >= 0) & (src < n) & ((row < t_lat) == (src < t_lat))
        shifted = x if d == 0 else pltpu.roll(x, (-d) % n, 0)
        acc += jnp.where(ok, shifted, 0.0) * w_ref[j:j + 1, :]
    o_ref[0] = _silu(acc)


def ssd_conv(proj, conv_w, conv_b, t_lat):
    nb, t, _ = proj.shape
    return pl.pallas_call(
        functools.partial(_conv_kernel, t_lat=t_lat),
        grid=(nb, SSD_XBC // LANES),
        in_specs=[pl.BlockSpec((1, t, LANES), lambda b, j: (b, 0, ODD_XBC // LANES + j)),
                  pl.BlockSpec((SSD_CONV, LANES), lambda b, j: (0, j)),
                  pl.BlockSpec((1, LANES), lambda b, j: (0, j))],
        out_specs=pl.BlockSpec((1, t, LANES), lambda b, j: (b, 0, j)),
        out_shape=jax.ShapeDtypeStruct((nb, t, SSD_XBC), F32),
        compiler_params=_params(("parallel", "parallel")),
        name="ssd_conv",
    )(proj, conv_w, conv_b.reshape(1, -1))


def _ssd_kernel(*refs, rev):
    if rev:
        (x_ref, bc_ref, gc_ref, gr_ref, dbr_ref, dbc_ref, alr_ref, alc_ref, prev_ref, z_ref, skip_ref, gain_ref,
         o_ref, s_ref, xw_ref, y_ref) = refs
    else:
        x_ref, bc_ref, gc_ref, gr_ref, dbr_ref, dbc_ref, alr_ref, alc_ref, o_ref, s_ref, xw_ref = refs
    L = SSD_CHUNK
    hpg = SSD_HEADS // SSD_GROUPS
    gw = hpg * SSD_HEAD_DIM

    d0 = SSD_HEADS if rev else 0
    dt_col = _softplus(gc_ref[0][:, d0:d0 + SSD_HEADS] + dbr_ref[:, d0:d0 + SSD_HEADS])
    dt_row = _softplus(gr_ref[0, 0][d0:d0 + SSD_HEADS, :] + dbc_ref[d0:d0 + SSD_HEADS, :])
    a_col = dt_col * -jnp.exp(alr_ref[:, d0:d0 + SSD_HEADS])
    a_row = dt_row * -jnp.exp(alc_ref[d0:d0 + SSD_HEADS, :])
    mask = _time_tri(L, rev)
    tri = mask.astype(F32)
    ac_col = _dot_f32(tri, a_col)
    ac_row = _dot_nt_f32(a_row, tri)
    last = 0 if rev else L - 1
    x = x_ref[0]
    bc = bc_ref[0]

    for g in range(SSD_GROUPS):
        bm = bc[:, g * SSD_STATE:(g + 1) * SSD_STATE]
        cm = bc[:, SSD_BC_W + g * SSD_STATE:SSD_BC_W + (g + 1) * SSD_STATE]
        cb = _dot_nt(cm, bm)
        state = s_ref[g]
        inter = _dot_nt(cm, state)
        for hh in range(hpg):
            h = g * hpg + hh
            cols = slice(h * SSD_HEAD_DIM, (h + 1) * SSD_HEAD_DIM)
            gcols = slice(hh * SSD_HEAD_DIM, (hh + 1) * SSD_HEAD_DIM)
            acol, arow = ac_col[:, h:h + 1], ac_row[h:h + 1, :]
            decay = jnp.exp(jnp.where(mask, acol - arow, NEG))
            xh = x[:, cols]
            y = _dot(cb * decay * dt_row[h:h + 1, :], xh) + jnp.exp(acol) * inter[:, gcols]
            a_last = acol[last:last + 1, :]
            xw_ref[:, gcols] = xh * (jnp.exp(a_last - acol) * dt_col[:, h:h + 1])
            s_ref[g, gcols, :] = state[gcols, :] * jnp.exp(a_last)
            if rev:
                y_ref[:, cols] = y + prev_ref[0][:, cols]
            else:
                o_ref[0, :, cols] = y
        s_ref[g] += _dot_tn(xw_ref[...], bm)

    if rev:
        yt = (y_ref[...] + skip_ref[...] * x) * _silu(z_ref[0])
        for g in range(SSD_GROUPS):
            cols = slice(g * gw, (g + 1) * gw)
            o_ref[0, :, cols] = (_rms(yt[:, cols]) * gain_ref[:, cols]).astype(o_ref.dtype)


def ssd(proj, gates, xbc, dt_bias, a_log, d_skip, ssd_gain, t_lat):
    nb, t, _ = proj.shape
    L = SSD_CHUNK
    n_lat, n_ctx = t_lat // L, (t - t_lat) // L
    gate_rows = _gate_rows(gates, L)
    skip = jnp.repeat(d_skip.astype(F32), SSD_HEAD_DIM).reshape(1, SSD_W)

    def call(rev, prev):
        idx = lambda b, s: _chunk_index(s, n_lat, n_ctx, rev)
        ns = SSD_SAMPLES_PER_STEP
        tok = lambda width, col: pl.BlockSpec((ns, L, width), lambda b, s: (b, idx(b, s), col // width))
        const = lambda shape: pl.BlockSpec(shape, lambda b, s: (0, 0))
        in_specs = [tok(SSD_W, 0), tok(2 * SSD_BC_W, SSD_W),
                    pl.BlockSpec((ns, L, LANES), lambda b, s: (b, idx(b, s), 0)),
                    pl.BlockSpec((ns, 1, GATE_ROWS, L), lambda b, s: (b, idx(b, s), 0, 0)),
                    const((1, LANES)), const((GATE_ROWS, 1)), const((1, LANES)), const((GATE_ROWS, 1))]
        args = [xbc, xbc, gates, gate_rows, _lane_row(dt_bias), _sublane_col(dt_bias),
                _lane_row(a_log), _sublane_col(a_log)]
        per_sample = [True] * 4 + [False] * 4
        scratch = [_sample_scratch(ns, (SSD_GROUPS, SSD_W // SSD_GROUPS, SSD_STATE)),
                   _sample_scratch(ns, (L, SSD_W // SSD_GROUPS))]
        if rev:
            in_specs += [tok(SSD_W, 0), tok(SSD_W, ODD_Z), const((1, SSD_W)), const((1, SSD_W))]
            args += [prev, proj, skip, ssd_gain.reshape(1, -1)]
            per_sample += [True, True, False, False]
            scratch += [_sample_scratch(ns, (L, SSD_W))]
        return pl.pallas_call(
            _for_each_sample(functools.partial(_ssd_kernel, rev=rev), per_sample + [True], len(scratch),
                             step_axis=1),
            grid=(nb // ns, n_lat + n_ctx),
            in_specs=in_specs,
            out_specs=tok(SSD_W, 0),
            out_shape=jax.ShapeDtypeStruct((nb, t, SSD_W), BF16 if rev else F32),
            scratch_shapes=scratch,
            compiler_params=_params(("parallel", "arbitrary")),
            name="ssd_bwd" if rev else "ssd_fwd",
        )(*args)

    return call(True, call(False, None))


def _split_w_in(w_in, n_gate):
    n_main = w_in.shape[1] - n_gate
    w_gate = jnp.zeros((w_in.shape[0], LANES), F32).at[:, :n_gate].set(w_in[:, n_main:])
    return w_in[:, :n_main].astype(BF16), w_gate.astype(BF16)


def kernel(x, c, ctx, c_ctx, hgrn_lb_logits,
           l0_mod_w, l0_mod_b, l0_w_in, l0_q_gain, l0_k_gain, l0_rpb, l0_gate_bias, l0_ml_gain,
           l0_w_out, l0_ffn_w1, l0_ffn_w3, l0_ffn_w2,
           l1_mod_w, l1_mod_b, l1_w_in, l1_hg_gain, l1_conv_w, l1_conv_b, l1_dt_bias, l1_a_log,
           l1_d_skip, l1_ssd_gain, l1_w_out, l1_ffn_w1, l1_ffn_w3, l1_ffn_w2):
    nb, t_lat, d = x.shape
    p = jax.nn.softmax(hgrn_lb_logits.astype(F32), axis=0)
    lower_bounds = jnp.cumsum(p, axis=0) - p[0]

    cvecs = jnp.zeros((8, d), F32).at[:nb].set(c).at[nb].set(c_ctx)
    xs = jnp.concatenate([x, ctx], axis=1)

    layers = [
        (l0_mod_w, l0_mod_b, l0_w_in, 4 * ML_HEADS, l0_w_out, l0_ffn_w1, l0_ffn_w3, l0_ffn_w2),
        (l1_mod_w, l1_mod_b, l1_w_in, 2 * SSD_HEADS, l1_w_out, l1_ffn_w1, l1_ffn_w3, l1_ffn_w2),
    ]
    for layer, (mod_w, mod_b, w_in, n_gate, w_out, w1, w3, w2) in enumerate(layers):
        mods = modulation_vectors(cvecs, mod_w, mod_b).reshape(8, N_MOD, 1, d)
        sh1, sc1, g1, sh2, sc2, g2 = [mods[:nb + 1, i] for i in range(N_MOD)]
        w_main, w_gate = _split_w_in(w_in, n_gate)
        proj, gates = modulated_in_proj(xs, sh1, sc1, w_main, w_gate, t_lat)
        if layer == 0:
            bias = na_bias_tables(l0_rpb, t_lat // GRID_W)
            ya = neighbourhood_attention(proj, bias, l0_q_gain, l0_k_gain, t_lat)
            yb = mlstm(proj, gates, l0_gate_bias, l0_ml_gain, t_lat)
        else:
            ya = hgrn2(proj, lower_bounds[layer], l1_hg_gain, t_lat)
            xbc = ssd_conv(proj, l1_conv_w, l1_conv_b, t_lat)
            yb = ssd(proj, gates, xbc, l1_dt_bias, l1_a_log, l1_d_skip, l1_ssd_gain, t_lat)
        ka = ya.shape[-1]
        t_out = xs.shape[1] if layer == 0 else t_lat
        xs = out_proj_residual(ya, yb, xs, g1, w_out[:ka].astype(BF16), w_out[ka:].astype(BF16), t_lat, t_out)
        xs = modulated_ffn_residual(xs, sh2, sc2, g2, w1.astype(BF16), w3.astype(BF16), w2.astype(BF16), t_lat)
    return xs
```
